```python
import math
import jax
import jax.numpy as jnp
from jax import lax
import numpy as np

D_MODEL = 1024
BATCH = 16
SEQ = 2048
DEPTH = 4

N_EVEN = (DEPTH + 1) // 2
N_ODD = DEPTH // 2

GLA_HEADS = 4
GLA_DV = D_MODEL // (2 * GLA_HEADS)
GLA_DK = GLA_DV // 2
GLA_RANK = 16
GLA_GATE_NORM = 16.0
GLA_CHUNK = 16

HGRN_HEADS = 4
HGRN_DV = D_MODEL // (2 * HGRN_HEADS)
HGRN_DK = HGRN_DV // 2
HGRN_CHUNK = 16
HGRN_MIN_F = 1e-20

RET_HEADS = 4
RET_DK = D_MODEL // (2 * RET_HEADS)
RET_DV = (3 * D_MODEL) // (4 * RET_HEADS)
RET_CHUNK = 64
ROPE_BASE = 10000.0

S5_WIDTH = D_MODEL // 4
S5_GROUP_CH = 16
S5_GROUPS = S5_WIDTH // S5_GROUP_CH
S5_STATE = 64

FFN_DIM = ((8 * D_MODEL // 3 + 255) // 256) * 256
CONV_WIDTH = 3
EPS = 1e-6

EVEN_COLS = (GLA_HEADS * GLA_DK, GLA_HEADS * GLA_DK, GLA_HEADS * GLA_DV, GLA_HEADS * GLA_DV, GLA_RANK, GLA_RANK,
             HGRN_HEADS * HGRN_DK, HGRN_HEADS * HGRN_DK, HGRN_HEADS * HGRN_DK, HGRN_HEADS * HGRN_DV, HGRN_HEADS * HGRN_DV)
ODD_COLS = (RET_HEADS * RET_DK, RET_HEADS * RET_DK, RET_HEADS * RET_DV, RET_HEADS * RET_DV, S5_WIDTH)
EVEN_IN = sum(EVEN_COLS)
ODD_IN = sum(ODD_COLS)
EVEN_MIX = GLA_HEADS * GLA_DV + HGRN_HEADS * HGRN_DV
ODD_MIX = RET_HEADS * RET_DV + S5_WIDTH

kernel_name = 'bidir_hybrid_gla_hgrn2_retnet_s5_convffn'

F32 = jnp.float32


def _split(p, sizes):
    return jnp.split(p, np.cumsum(sizes)[:-1].tolist(), axis=-1)


def rmsnorm(x, g):
    xf = x.astype(F32)
    y = xf * lax.rsqrt(jnp.mean(xf * xf, axis=-1, keepdims=True) + EPS)
    return (y * g.astype(F32)).astype(x.dtype)


def to_heads(t, h):
    b, s, _ = t.shape
    return t.reshape(b, s, h, -1).transpose(0, 2, 1, 3).astype(F32)


def from_heads(t):
    return t.transpose(0, 2, 1, 3)


def head_rmsnorm(o, g):
    b, s, h, d = o.shape
    y = o * lax.rsqrt(jnp.mean(o * o, axis=-1, keepdims=True) + EPS)
    return (y * g.astype(F32).reshape(h, d)).reshape(b, s, h * d)


def head_groupnorm(o, g):
    b, s, h, d = o.shape
    mu = jnp.mean(o, axis=-1, keepdims=True)
    c = o - mu
    y = c * lax.rsqrt(jnp.mean(c * c, axis=-1, keepdims=True) + EPS)
    return (y * g.astype(F32).reshape(h, d)).reshape(b, s, h * d)


def chunk_gated_scan(q, k, v, log_a, chunk):
    b, h, s, dk = q.shape
    dv = v.shape[-1]
    dg = log_a.shape[-1]
    n = s // chunk
    q, k, v, log_a = (t.reshape(b, h, n, chunk, t.shape[-1]) for t in (q, k, v, log_a))
    cum = jnp.cumsum(log_a, axis=3)
    last = cum[:, :, :, -1:, :]
    pos = jnp.arange(chunk)
    lower = (pos[:, None] >= pos[None, :])[:, :, None]
    rel = cum[:, :, :, :, None, :] - cum[:, :, :, None, :, :]
    decay = jnp.where(lower, jnp.exp(jnp.where(lower, rel, 0.0)), 0.0)
    if dg == 1:
        scores = jnp.einsum('bhnid,bhnjd->bhnij', q, k) * decay[..., 0]
    else:
        scores = jnp.einsum('bhnid,bhnjd,bhnijd->bhnij', q, k, decay)
    o_intra = jnp.einsum('bhnij,bhnjv->bhniv', scores, v)
    q_in = q * jnp.exp(cum)
    k_out = k * jnp.exp(last - cum)
    d_state = jnp.einsum('bhnid,bhniv->bhndv', k_out, v)
    chunk_decay = jnp.exp(last[:, :, :, 0, :])

    def step(state, inp):
        g_c, ds_c = inp
        return g_c[..., None] * state + ds_c, state

    init = jnp.zeros((b, h, dk, dv), q.dtype)
    _, prev = lax.scan(step, init, (jnp.moveaxis(chunk_decay, 2, 0), jnp.moveaxis(d_state, 2, 0)))
    prev = jnp.moveaxis(prev, 0, 2)
    o_inter = jnp.einsum('bhnid,bhndv->bhniv', q_in, prev)
    return (o_intra + o_inter).reshape(b, h, s, dv)


def bidir_scan(q, k_f, k_b, v, la_f, la_b, chunk):
    flip = lambda t: jnp.flip(t, axis=2)
    fwd = chunk_gated_scan(q, k_f, v, la_f, chunk)
    bwd = chunk_gated_scan(flip(q), flip(k_b), flip(v), flip(la_b), chunk)
    return fwd + flip(bwd)


def gla_mixer(q, k, v, r, lr_f, lr_b, wa2, ba, norm_g):
    q = to_heads(q, GLA_HEADS)
    k = to_heads(k, GLA_HEADS) * (GLA_DK ** -0.5)
    v = to_heads(v, GLA_HEADS)

    def log_gate(lr, d):
        z = jnp.einsum('bsr,rk->bsk', lr, wa2[d]) + ba[d]
        return to_heads(jax.nn.log_sigmoid(z.astype(F32)) / GLA_GATE_NORM, GLA_HEADS)

    o = bidir_scan(q, k, k, v, log_gate(lr_f, 0), log_gate(lr_b, 1), GLA_CHUNK)
    o = head_rmsnorm(from_heads(o), norm_g)
    return o * jax.nn.silu(r.astype(F32))


def hgrn_lower_bounds(lb_logits):
    p = jax.nn.softmax(lb_logits.astype(F32), axis=1)
    return jnp.cumsum(p, axis=1) - p[:, :1]


def hgrn2_mixer(q, z_f, z_b, i, g, lb_f, lb_b, norm_g):
    q = jax.nn.silu(to_heads(q, HGRN_HEADS))
    v = to_heads(i, HGRN_HEADS)

    def gate(z, lb):
        z = z.astype(F32)
        f = lb + (1.0 - lb) * jax.nn.sigmoid(z)
        log_f = jnp.log(jnp.maximum(f, HGRN_MIN_F))
        key = (1.0 - lb) * jax.nn.sigmoid(-z)
        return to_heads(log_f, HGRN_HEADS), to_heads(key, HGRN_HEADS)

    la_f, k_f = gate(z_f, lb_f)
    la_b, k_b = gate(z_b, lb_b)
    o = bidir_scan(q, k_f, k_b, v, la_f, la_b, HGRN_CHUNK)
    o = head_rmsnorm(from_heads(o), norm_g)
    return o * jax.nn.silu(g.astype(F32))


def rotary(t):
    s, d = t.shape[2], t.shape[3]
    half = d // 2
    inv = ROPE_BASE ** (-jnp.arange(half, dtype=F32) / half)
    ang = jnp.arange(s, dtype=F32)[:, None] * inv[None, :]
    cos, sin = jnp.cos(ang), jnp.sin(ang)
    t1, t2 = t[..., :half], t[..., half:]
    return jnp.concatenate([t1 * cos - t2 * sin, t1 * sin + t2 * cos], axis=-1)


def retention_mixer(q, k, v, g, norm_g):
    q = rotary(to_heads(q, RET_HEADS))
    k = rotary(to_heads(k, RET_HEADS)) * (RET_DK ** -0.5)
    v = to_heads(v, RET_HEADS)
    b, h, s, _ = q.shape
    hidx = jnp.arange(RET_HEADS, dtype=F32)
    log_gamma_f = jnp.log1p(-jnp.exp2(-5.0 - hidx))
    log_gamma_b = jnp.log1p(-jnp.exp2(-5.5 - hidx))
    la_f = jnp.broadcast_to(log_gamma_f[None, :, None, None], (b, h, s, 1))
    la_b = jnp.broadcast_to(log_gamma_b[None, :, None, None], (b, h, s, 1))
    o = bidir_scan(q, k, k, v, la_f, la_b, RET_CHUNK)
    o = head_groupnorm(from_heads(o), norm_g)
    return o * jax.nn.silu(g.astype(F32))


def _complex_affine_combine(e1, e2):
    a1r, a1i, b1r, b1i = e1
    a2r, a2i, b2r, b2i = e2
    return (a1r * a2r - a1i * a2i,
            a1r * a2i + a1i * a2r,
            a2r * b1r - a2i * b1i + b2r,
            a2r * b1i + a2i * b1r + b2i)


def _s5_direction(ug, lam_re, lam_im, log_dt, b_re, b_im, reverse):
    s = ug.shape[1]
    lr = jnp.minimum(lam_re.astype(F32), -1e-4)
    li = lam_im.astype(F32)
    dt = jnp.exp(log_dt.astype(F32))[:, None]
    mag = jnp.exp(lr * dt)
    ar, ai = mag * jnp.cos(li * dt), mag * jnp.sin(li * dt)
    den = lr * lr + li * li
    nr = ar - 1.0
    cr = (nr * lr + ai * li) / den
    ci = (ai * lr - nr * li) / den
    br, bi = b_re.astype(F32), b_im.astype(F32)
    bbr = cr[..., None] * br - ci[..., None] * bi
    bbi = cr[..., None] * bi + ci[..., None] * br
    xr = jnp.einsum('bsgp,gnp->bsgn', ug, bbr)
    xi = jnp.einsum('bsgp,gnp->bsgn', ug, bbi)
    g_, n_ = ar.shape
    a_r = jnp.broadcast_to(ar[None, None], (1, s, g_, n_))
    a_i = jnp.broadcast_to(ai[None, None], (1, s, g_, n_))
    _, _, sr, si = lax.associative_scan(_complex_affine_combine, (a_r, a_i, xr, xi), reverse=reverse, axis=1)
    return sr, si


def s5_mixer(u, lam_re, lam_im, log_dt, b_re, b_im, c_re, c_im, d_skip, glu_w, glu_b):
    bsz, s, _ = u.shape
    uf = u.astype(F32)
    ug = uf.reshape(bsz, s, S5_GROUPS, S5_GROUP_CH)
    fr, fi = _s5_direction(ug, lam_re[0], lam_im[0], log_dt[0], b_re, b_im, False)
    rr, ri = _s5_direction(ug, lam_re[1], lam_im[1], log_dt[1], b_re, b_im, True)
    hr, hi = fr + rr, fi + ri
    y = (jnp.einsum('bsgn,gpn->bsgp', hr, c_re.astype(F32))
         - jnp.einsum('bsgn,gpn->bsgp', hi, c_im.astype(F32)))
    y = y.reshape(bsz, s, S5_WIDTH) + d_skip.astype(F32) * uf
    g = jax.nn.gelu(y)
    return g * jax.nn.sigmoid(jnp.einsum('bsc,ce->bse', g, glu_w.astype(F32)) + glu_b.astype(F32))


def even_mixer(h, w_in, w_out, wa2, ba, gla_g, lb_f, lb_b, hgrn_g):
    p = jnp.einsum('bsd,de->bse', h, w_in)
    gq, gk, gv, gr, glf, glb, hq, hzf, hzb, hi, hg = _split(p, EVEN_COLS)
    a = gla_mixer(gq, gk, gv, gr, glf, glb, wa2, ba, gla_g)
    bm = hgrn2_mixer(hq, hzf, hzb, hi, hg, lb_f, lb_b, hgrn_g)
    y = jnp.concatenate([a, bm], axis=-1).astype(h.dtype)
    return jnp.einsum('bse,ed->bsd', y, w_out)


def odd_mixer(h, w_in, w_out, ret_g, lam_re, lam_im, log_dt, b_re, b_im, c_re, c_im, d_skip, glu_w, glu_b):
    p = jnp.einsum('bsd,de->bse', h, w_in)
    rq, rk, rv, rg, su = _split(p, ODD_COLS)
    c = retention_mixer(rq, rk, rv, rg, ret_g)
    dm = s5_mixer(su, lam_re, lam_im, log_dt, b_re, b_im, c_re, c_im, d_skip, glu_w, glu_b)
    y = jnp.concatenate([c, dm], axis=-1).astype(h.dtype)
    return jnp.einsum('bse,ed->bsd', y, w_out)


def conv_ffn(h, w_up, conv_w, conv_b, w_down):
    u = jnp.einsum('bsd,df->bsf', h, w_up)
    s = u.shape[1]
    pad = CONV_WIDTH // 2
    up = jnp.pad(u, ((0, 0), (pad, pad), (0, 0)))
    c = conv_b + up[:, 0:s] * conv_w[0]
    for t in range(1, CONV_WIDTH):
        c = c + up[:, t:t + s] * conv_w[t]
    a, v = jnp.split(c, 2, axis=-1)
    return jnp.einsum('bsf,fd->bsd', jax.nn.silu(a) * v, w_down)


def setup_inputs(seed: int = 0) -> dict:
    key = jax.random.key(seed)
    keys = iter(jax.random.split(key, 40))

    def nrm(shape, scale=1.0):
        return scale * jax.random.normal(next(keys), shape, F32)

    def gain(shape):
        return 1.0 + 0.01 * nrm(shape)

    gla_hk = GLA_HEADS * GLA_DK
    hgrn_hk = HGRN_HEADS * HGRN_DK
    x = nrm((BATCH, SEQ, D_MODEL))
    mix_norm_g = gain((DEPTH, D_MODEL))
    ffn_norm_g = gain((DEPTH, D_MODEL))
    final_norm_g = gain((D_MODEL,))
    w_in_even = nrm((N_EVEN, D_MODEL, EVEN_IN), D_MODEL ** -0.5)
    w_out_even = nrm((N_EVEN, EVEN_MIX, D_MODEL), EVEN_MIX ** -0.5)
    gla_wa2 = nrm((N_EVEN, 2, GLA_RANK, gla_hk), GLA_RANK ** -0.5)
    gla_ba = nrm((N_EVEN, 2, gla_hk), 0.1)
    gla_norm_g = gain((N_EVEN, GLA_HEADS * GLA_DV))
    hgrn_lb_logits = nrm((2, N_EVEN, hgrn_hk), 0.1)
    hgrn_norm_g = gain((N_EVEN, HGRN_HEADS * HGRN_DV))
    w_in_odd = nrm((N_ODD, D_MODEL, ODD_IN), D_MODEL ** -0.5)
    w_out_odd = nrm((N_ODD, ODD_MIX, D_MODEL), ODD_MIX ** -0.5)
    ret_norm_g = gain((N_ODD, RET_HEADS * RET_DV))
    s5_lam_re = -0.5 + 0.01 * nrm((N_ODD, 2, S5_GROUPS, S5_STATE))
    s5_lam_im = jnp.pi * jnp.arange(S5_STATE, dtype=F32) + 0.01 * nrm((N_ODD, 2, S5_GROUPS, S5_STATE))
    s5_log_dt = jax.random.uniform(next(keys), (N_ODD, 2, S5_GROUPS), F32, math.log(1e-3), math.log(1e-1))
    s5_b_re = nrm((N_ODD, S5_GROUPS, S5_STATE, S5_GROUP_CH), (2.0 * S5_GROUP_CH) ** -0.5)
    s5_b_im = nrm((N_ODD, S5_GROUPS, S5_STATE, S5_GROUP_CH), (2.0 * S5_GROUP_CH) ** -0.5)
    s5_c_re = nrm((N_ODD, S5_GROUPS, S5_GROUP_CH, S5_STATE), S5_STATE ** -0.5)
    s5_c_im = nrm((N_ODD, S5_GROUPS, S5_GROUP_CH, S5_STATE), S5_STATE ** -0.5)
    s5_d = nrm((N_ODD, S5_WIDTH))
    s5_glu_w = nrm((N_ODD, S5_WIDTH, S5_WIDTH), S5_WIDTH ** -0.5)
    s5_glu_b = nrm((N_ODD, S5_WIDTH), 0.01)
    ffn_w_up = nrm((DEPTH, D_MODEL, 2 * FFN_DIM), D_MODEL ** -0.5)
    ffn_conv_w = nrm((DEPTH, CONV_WIDTH, 2 * FFN_DIM), CONV_WIDTH ** -0.5)
    ffn_conv_b = nrm((DEPTH, 2 * FFN_DIM), 0.01)
    ffn_w_down = nrm((DEPTH, FFN_DIM, D_MODEL), FFN_DIM ** -0.5)
    return {'x': x, 'mix_norm_g': mix_norm_g, 'ffn_norm_g': ffn_norm_g, 'final_norm_g': final_norm_g,
            'w_in_even': w_in_even, 'w_out_even': w_out_even, 'gla_wa2': gla_wa2, 'gla_ba': gla_ba,
            'gla_norm_g': gla_norm_g, 'hgrn_lb_logits': hgrn_lb_logits, 'hgrn_norm_g': hgrn_norm_g,
            'w_in_odd': w_in_odd, 'w_out_odd': w_out_odd, 'ret_norm_g': ret_norm_g,
            's5_lam_re': s5_lam_re, 's5_lam_im': s5_lam_im, 's5_log_dt': s5_log_dt,
            's5_b_re': s5_b_re, 's5_b_im': s5_b_im, 's5_c_re': s5_c_re, 's5_c_im': s5_c_im,
            's5_d': s5_d, 's5_glu_w': s5_glu_w, 's5_glu_b': s5_glu_b,
            'ffn_w_up': ffn_w_up, 'ffn_conv_w': ffn_conv_w, 'ffn_conv_b': ffn_conv_b, 'ffn_w_down': ffn_w_down}


def reference(x, mix_norm_g, ffn_norm_g, final_norm_g,
              w_in_even, w_out_even, gla_wa2, gla_ba, gla_norm_g, hgrn_lb_logits, hgrn_norm_g,
              w_in_odd, w_out_odd, ret_norm_g, s5_lam_re, s5_lam_im, s5_log_dt,
              s5_b_re, s5_b_im, s5_c_re, s5_c_im, s5_d, s5_glu_w, s5_glu_b,
              ffn_w_up, ffn_conv_w, ffn_conv_b, ffn_w_down):
    lbs = hgrn_lower_bounds(hgrn_lb_logits)
    for layer in range(DEPTH):
        j = layer // 2
        h = rmsnorm(x, mix_norm_g[layer])
        if layer % 2 == 0:
            mix = even_mixer(h, w_in_even[j], w_out_even[j], gla_wa2[j], gla_ba[j], gla_norm_g[j],
                             lbs[0, j], lbs[1, j], hgrn_norm_g[j])
        else:
            mix = odd_mixer(h, w_in_odd[j], w_out_odd[j], ret_norm_g[j], s5_lam_re[j], s5_lam_im[j],
                            s5_log_dt[j], s5_b_re[j], s5_b_im[j], s5_c_re[j], s5_c_im[j], s5_d[j],
                            s5_glu_w[j], s5_glu_b[j])
        x = x + mix.astype(x.dtype)
        hf = rmsnorm(x, ffn_norm_g[layer])
        x = x + conv_ffn(hf, ffn_w_up[layer], ffn_conv_w[layer], ffn_conv_b[layer], ffn_w_down[layer]).astype(x.dtype)
    return rmsnorm(x, final_norm_g)
```

```python
import functools
import math

import jax
import jax.numpy as jnp
import numpy as np
from jax import lax
from jax.experimental import pallas as pl
from jax.experimental.pallas import tpu as pltpu

F32 = jnp.float32
BF16 = jnp.bfloat16
HIGHEST = lax.Precision.HIGHEST

D_MODEL = 1024
DEPTH = 4
EPS = 1e-6

GLA_HEADS = 4
GLA_DK = 64
GLA_DV = 128
GLA_RANK = 16
GLA_GATE_NORM = 16.0
HGRN_HEADS = 4
HGRN_DK = 64
HGRN_DV = 128
HGRN_MIN_F = 1e-20

RET_HEADS = 4
RET_DK = 128
RET_DV = 192
RET_DV_PAD = 256
ROPE_BASE = 10000.0

S5_WIDTH = 256
S5_GROUP_CH = 16
S5_GROUPS = 16
S5_STATE = 64
S5_CHUNK = 16

FFN_DIM = 2816
FFN_CHUNK = 256
CONV_WIDTH = 3

LANE = 128
SUBLANE = 8
GATED_CHUNK = 64
RET_CHUNK = 256
ROW_TILE = 512
VMEM_LIMIT = 56 * 1024 * 1024

NT_DIMS = (((1,), (1,)), ((), ()))
TN_DIMS = (((0,), (0,)), ((), ()))


def _cparams(n_axes):
    return pltpu.CompilerParams(dimension_semantics=("arbitrary",) * n_axes,
                                vmem_limit_bytes=VMEM_LIMIT)


def _rms(x, g):
    return x * lax.rsqrt(jnp.mean(x * x, axis=-1, keepdims=True) + EPS) * g


def _sigmoid(x):
    return 1.0 / (1.0 + jnp.exp(-x))


def _silu(x):
    return x * _sigmoid(x)


def _log_sigmoid(z):
    return jnp.minimum(z, 0.0) - jnp.log(1.0 + jnp.exp(-jnp.abs(z)))


def _gelu_tanh(x):
    c = math.sqrt(2.0 / math.pi)
    return 0.5 * x * (1.0 + jnp.tanh(c * (x + 0.044715 * (x * x * x))))


def _norm_matmul_kernel(x_ref, g_ref, w_ref, o_ref, *, col_chunk):
    h = _rms(x_ref[...], g_ref[...]).astype(BF16)
    n_out = o_ref.shape[1]
    for j in range(n_out // col_chunk):
        cols = slice(j * col_chunk, (j + 1) * col_chunk)
        o_ref[:, cols] = jnp.dot(h, w_ref[:, cols], preferred_element_type=F32)


def _norm_matmul(x2d, g, w):
    t, d = x2d.shape
    e = w.shape[1]
    col_chunk = next(c * LANE for c in (4, 3, 2, 1) if e % (c * LANE) == 0)
    return pl.pallas_call(
        functools.partial(_norm_matmul_kernel, col_chunk=col_chunk),
        grid=(t // ROW_TILE,),
        in_specs=[pl.BlockSpec((ROW_TILE, d), lambda i: (i, 0)),
                  pl.BlockSpec((1, d), lambda i: (0, 0)),
                  pl.BlockSpec((d, e), lambda i: (0, 0))],
        out_specs=pl.BlockSpec((ROW_TILE, e), lambda i: (i, 0)),
        out_shape=jax.ShapeDtypeStruct((t, e), F32),
        compiler_params=_cparams(1),
        name="norm_in_proj",
    )(x2d, g.reshape(1, d), w)


def _gated_scan(qs, kf, kb, laf, lab, v_ref, o_acc, st_f, st_b, *, seq, chunk):
    n_chunks = seq // chunk
    row = lax.broadcasted_iota(jnp.int32, (chunk, chunk), 0)
    col = lax.broadcasted_iota(jnp.int32, (chunk, chunk), 1)
    lower = row >= col
    upper = row <= col
    lower_f = lower.astype(F32)
    upper_f = upper.astype(F32)
    lane = lax.broadcasted_iota(jnp.int32, (1, LANE), 1)
    head_mask = [(lane < GLA_DK).astype(F32), (lane >= GLA_DK).astype(F32)]
    mid = chunk // 2

    o_acc[...] = jnp.zeros_like(o_acc)
    st_f[...] = jnp.zeros_like(st_f)
    st_b[...] = jnp.zeros_like(st_b)

    def visit(c, forward):
        rows = pl.ds(pl.multiple_of(c * chunk, chunk), chunk)
        tri = lower if forward else upper
        la = (laf if forward else lab)[rows, :]
        cum = jnp.dot(lower_f if forward else upper_f, la, precision=HIGHEST, preferred_element_type=F32)
        tot = cum[chunk - 1:chunk, :] if forward else cum[0:1, :]
        ref = cum[mid:mid + 1, :]
        q_t = qs[rows, :] * jnp.exp(cum - ref)
        k_t = (kf if forward else kb)[rows, :] * jnp.exp(ref - cum)
        q_h = q_t * jnp.exp(ref)
        k_h = (k_t * jnp.exp(tot - ref)).astype(BF16)
        k_t = k_t.astype(BF16)
        decay = jnp.exp(tot)
        st = st_f if forward else st_b
        v = v_ref[rows, :]
        outs = []
        for h in range(2):
            v_h = v[:, h * LANE:(h + 1) * LANE].astype(BF16)
            a = lax.dot_general((q_t * head_mask[h]).astype(BF16), k_t, NT_DIMS,
                                preferred_element_type=F32)
            a = jnp.where(tri, a, 0.0).astype(BF16)
            o = jnp.dot(a, v_h, preferred_element_type=F32)
            s_old = st[h]
            o = o + lax.dot_general((q_h * head_mask[h]).astype(BF16), s_old.astype(BF16),
                                    NT_DIMS, preferred_element_type=F32)
            st[h] = s_old * decay + lax.dot_general(v_h, k_h, TN_DIMS, preferred_element_type=F32)
            outs.append(o)
        o_acc[rows, :] = o_acc[rows, :] + jnp.concatenate(outs, axis=1)

    def body(c, carry):
        visit(c, True)
        visit(n_chunks - 1 - c, False)
        return carry

    lax.fori_loop(0, n_chunks, body, 0)


def _head_rmsnorm_gate(o_acc, gate_ref, g_ref, y_ref):
    o = o_acc[...]
    gate = gate_ref[...]
    g = g_ref[...]
    outs = []
    for h in range(2):
        cols = slice(h * LANE, (h + 1) * LANE)
        outs.append(_rms(o[:, cols], g[:, cols]) * _silu(gate[:, cols]))
    y_ref[...] = jnp.concatenate(outs, axis=1).astype(y_ref.dtype)


def _gla_pair_kernel(q_ref, k_ref, v_ref, r_ref, lr_ref, wgf_ref, wgb_ref, baf_ref, bab_ref, g_ref,
                     y_ref, qs, kk, laf, lab, o_acc, st_f, st_b, *, seq, chunk):
    qs[...] = q_ref[...]
    kk[...] = k_ref[...] * (GLA_DK ** -0.5)
    lr = lr_ref[...]
    zf = jnp.dot(lr, wgf_ref[...], precision=HIGHEST, preferred_element_type=F32) + baf_ref[...]
    laf[...] = _log_sigmoid(zf) * (1.0 / GLA_GATE_NORM)
    zb = jnp.dot(lr, wgb_ref[...], precision=HIGHEST, preferred_element_type=F32) + bab_ref[...]
    lab[...] = _log_sigmoid(zb) * (1.0 / GLA_GATE_NORM)
    _gated_scan(qs, kk, kk, laf, lab, v_ref, o_acc, st_f, st_b, seq=seq, chunk=chunk)
    _head_rmsnorm_gate(o_acc, r_ref, g_ref, y_ref)


def _hgrn_pair_kernel(q_ref, zf_ref, zb_ref, v_ref, gate_ref, lbf_ref, lbb_ref, g_ref,
                      y_ref, qs, kf, kb, laf, lab, o_acc, st_f, st_b, *, seq, chunk):
    qs[...] = _silu(q_ref[...])
    for z_ref, lb_ref, k_out, la_out in ((zf_ref, lbf_ref, kf, laf), (zb_ref, lbb_ref, kb, lab)):
        z = z_ref[...]
        lb = lb_ref[...]
        f = lb + (1.0 - lb) * _sigmoid(z)
        la_out[...] = jnp.log(jnp.maximum(f, HGRN_MIN_F))
        k_out[...] = (1.0 - lb) * _sigmoid(-z)
    _gated_scan(qs, kf, kb, laf, lab, v_ref, o_acc, st_f, st_b, seq=seq, chunk=chunk)
    _head_rmsnorm_gate(o_acc, gate_ref, g_ref, y_ref)


def _col_spec(seq, width, block_fn):
    return pl.BlockSpec((None, seq, width), lambda b, p: (b, 0, block_fn(p)))


def _pair_spec(shape):
    return pl.BlockSpec((None,) + shape, lambda b, p: (p,) + (0,) * len(shape))


def _gated_scratch(seq, n_key_bufs):
    bufs = [pltpu.VMEM((seq, LANE), F32) for _ in range(1 + n_key_bufs + 2)]
    return bufs + [pltpu.VMEM((seq, 2 * LANE), F32),
                   pltpu.VMEM((2, LANE, LANE), F32), pltpu.VMEM((2, LANE, LANE), F32)]


EVEN_BLK = dict(gq=0, gk=2, gv=4, gr=8, hq=12, hzf=14, hzb=16, hi=18, hg=22, lr=26)
EVEN_COLS_PADDED = 27 * LANE


def _gla_mixer(p3, wgf, wgb, baf, bab, norm_g):
    b, seq, _ = p3.shape
    blk = EVEN_BLK
    kern = functools.partial(_gla_pair_kernel, seq=seq, chunk=GATED_CHUNK)
    return pl.pallas_call(
        kern, grid=(b, 2),
        in_specs=[_col_spec(seq, LANE, lambda p: blk["gq"] + p),
                  _col_spec(seq, LANE, lambda p: blk["gk"] + p),
                  _col_spec(seq, 2 * LANE, lambda p: blk["gv"] // 2 + p),
                  _col_spec(seq, 2 * LANE, lambda p: blk["gr"] // 2 + p),
                  _col_spec(seq, LANE, lambda p: blk["lr"]),
                  _pair_spec((LANE, LANE)), _pair_spec((LANE, LANE)),
                  _pair_spec((1, LANE)), _pair_spec((1, LANE)), _pair_spec((1, 2 * LANE))],
        out_specs=pl.BlockSpec((None, seq, 2 * LANE), lambda b, p: (b, 0, p)),
        out_shape=jax.ShapeDtypeStruct((b, seq, GLA_HEADS * GLA_DV), BF16),
        scratch_shapes=_gated_scratch(seq, 1),
        compiler_params=_cparams(2),
        name="gla_mixer",
    )(p3, p3, p3, p3, p3, wgf, wgb, baf, bab, norm_g)


def _hgrn_mixer(p3, lbf, lbb, norm_g):
    b, seq, _ = p3.shape
    blk = EVEN_BLK
    kern = functools.partial(_hgrn_pair_kernel, seq=seq, chunk=GATED_CHUNK)
    return pl.pallas_call(
        kern, grid=(b, 2),
        in_specs=[_col_spec(seq, LANE, lambda p: blk["hq"] + p),
                  _col_spec(seq, LANE, lambda p: blk["hzf"] + p),
                  _col_spec(seq, LANE, lambda p: blk["hzb"] + p),
                  _col_spec(seq, 2 * LANE, lambda p: blk["hi"] // 2 + p),
                  _col_spec(seq, 2 * LANE, lambda p: blk["hg"] // 2 + p),
                  _pair_spec((1, LANE)), _pair_spec((1, LANE)), _pair_spec((1, 2 * LANE))],
        out_specs=pl.BlockSpec((None, seq, 2 * LANE), lambda b, p: (b, 0, p)),
        out_shape=jax.ShapeDtypeStruct((b, seq, HGRN_HEADS * HGRN_DV), BF16),
        scratch_shapes=_gated_scratch(seq, 2),
        compiler_params=_cparams(2),
        name="hgrn2_mixer",
    )(p3, p3, p3, p3, p3, lbf, lbb, norm_g)


def _retention_kernel(q_ref, k_ref, v_ref, gate_ref, cos_ref, sin_ref, lg_ref, g_ref,
                      y_ref, qr, kr, o_acc, s_f, s_b, *, seq, chunk):
    half = RET_DK // 2
    cos = cos_ref[...]
    sin = sin_ref[...]
    q = q_ref[...]
    k = k_ref[...]
    qr[...] = q * cos + pltpu.roll(q, half, 1) * sin
    kr[...] = (k * cos + pltpu.roll(k, half, 1) * sin) * (RET_DK ** -0.5)

    lg_f = lg_ref[0:1, :]
    lg_b = lg_ref[1:2, :]
    n_chunks = seq // chunk
    row = lax.broadcasted_iota(jnp.int32, (chunk, chunk), 0)
    col = lax.broadcasted_iota(jnp.int32, (chunk, chunk), 1)
    diff = (row - col).astype(F32)
    dmat = (jnp.where(row >= col, jnp.exp(lg_f * diff), 0.0)
            + jnp.where(row <= col, jnp.exp(-lg_b * diff), 0.0))
    pos = lax.broadcasted_iota(jnp.int32, (chunk, LANE), 0).astype(F32)
    lgf = lg_f[:, 0:LANE]
    lgb = lg_b[:, 0:LANE]
    qdec_f = jnp.exp(lgf * (pos + 1.0))
    kdec_f = jnp.exp(lgf * (chunk - 1.0 - pos))
    qdec_b = jnp.exp(lgb * (chunk - pos))
    kdec_b = jnp.exp(lgb * pos)
    tot_f = jnp.exp(lgf[:, 0:1] * float(chunk))
    tot_b = jnp.exp(lgb[:, 0:1] * float(chunk))

    s_f[...] = jnp.zeros_like(s_f)
    s_b[...] = jnp.zeros_like(s_b)

    def body(c, carry):
        rows = pl.ds(pl.multiple_of(c * chunk, chunk), chunk)
        qc = qr[rows, :]
        kc = kr[rows, :]
        vc = v_ref[rows, :].astype(BF16)
        a = lax.dot_general(qc.astype(BF16), kc.astype(BF16), NT_DIMS, preferred_element_type=F32)
        o = jnp.dot((a * dmat).astype(BF16), vc, preferred_element_type=F32)
        s_old = s_f[...]
        o = o + jnp.dot((qc * qdec_f).astype(BF16), s_old.astype(BF16), preferred_element_type=F32)
        s_f[...] = s_old * tot_f + lax.dot_general((kc * kdec_f).astype(BF16), vc, TN_DIMS,
                                                   preferred_element_type=F32)
        o_acc[rows, :] = o
        return carry

    def body_b(c, carry):
        rows = pl.ds(pl.multiple_of((n_chunks - 1 - c) * chunk, chunk), chunk)
        qc = qr[rows, :]
        kc = kr[rows, :]
        vc = v_ref[rows, :].astype(BF16)
        s_old = s_b[...]
        o = jnp.dot((qc * qdec_b).astype(BF16), s_old.astype(BF16), preferred_element_type=F32)
        s_b[...] = s_old * tot_b + lax.dot_general((kc * kdec_b).astype(BF16), vc, TN_DIMS,
                                                   preferred_element_type=F32)
        o_acc[rows, :] = o_acc[rows, :] + o
        return carry

    lax.fori_loop(0, n_chunks, body, 0)
    lax.fori_loop(0, n_chunks, body_b, 0)

    o = o_acc[...]
    vlane = lax.broadcasted_iota(jnp.int32, (1, RET_DV_PAD), 1) < RET_DV
    mu = jnp.sum(o, axis=-1, keepdims=True) * (1.0 / RET_DV)
    cen = jnp.where(vlane, o - mu, 0.0)
    var = jnp.sum(cen * cen, axis=-1, keepdims=True) * (1.0 / RET_DV)
    y = cen * lax.rsqrt(var + EPS) * g_ref[...]
    y_ref[...] = (y * _silu(gate_ref[...])).astype(y_ref.dtype)


ODD_BLK = dict(rq=0, rk=4, rv=8, rg=16, su=24)
ODD_COLS_PADDED = 26 * LANE


def _retention_mixer(p3, cos2, sin2, log_gamma, norm_g):
    b, seq, _ = p3.shape
    blk = ODD_BLK
    chunk = min(RET_CHUNK, seq)
    kern = functools.partial(_retention_kernel, seq=seq, chunk=chunk)
    head = lambda shape: pl.BlockSpec((None,) + shape, lambda b, h: (h,) + (0,) * len(shape))
    table = pl.BlockSpec((seq, LANE), lambda b, h: (0, 0))
    return pl.pallas_call(
        kern, grid=(b, RET_HEADS),
        in_specs=[pl.BlockSpec((None, seq, LANE), lambda b, h: (b, 0, blk["rq"] + h)),
                  pl.BlockSpec((None, seq, LANE), lambda b, h: (b, 0, blk["rk"] + h)),
                  pl.BlockSpec((None, seq, RET_DV_PAD), lambda b, h: (b, 0, blk["rv"] // 2 + h)),
                  pl.BlockSpec((None, seq, RET_DV_PAD), lambda b, h: (b, 0, blk["rg"] // 2 + h)),
                  table, table, head((2, chunk)), head((1, RET_DV_PAD))],
        out_specs=pl.BlockSpec((None, seq, RET_DV_PAD), lambda b, h: (b, 0, h)),
        out_shape=jax.ShapeDtypeStruct((b, seq, RET_HEADS * RET_DV_PAD), BF16),
        scratch_shapes=[pltpu.VMEM((seq, LANE), F32), pltpu.VMEM((seq, LANE), F32),
                        pltpu.VMEM((seq, RET_DV_PAD), F32),
                        pltpu.VMEM((RET_DK, RET_DV_PAD), F32), pltpu.VMEM((RET_DK, RET_DV_PAD), F32)],
        compiler_params=_cparams(2),
        name="retention_mixer",
    )(p3, p3, p3, p3, cos2, sin2, log_gamma, norm_g)


def _s5_kernel(u_ref, m_ref, bst_ref, cst_ref, lam_ref, y_ref, *, n_chunks):
    u = u_ref[...].astype(BF16)
    x = jnp.dot(u, bst_ref[...], preferred_element_type=F32)
    half = S5_STATE
    row = lax.broadcasted_iota(jnp.int32, (n_chunks, LANE), 0)
    lane = lax.broadcasted_iota(jnp.int32, (1, LANE), 1)
    sign = jnp.where(lane < half, -1.0, 1.0)

    def cmul(z, a_re, a_im_signed):
        return z * a_re + pltpu.roll(z, half, 1) * a_im_signed

    hs = []
    for d in range(2):
        z = x[:, d * LANE:(d + 1) * LANE]
        a_re = lam_ref[2 * d:2 * d + 1, :]
        a_im = lam_ref[2 * d + 1:2 * d + 2, :] * sign
        step = 1
        while step < n_chunks:
            if d == 0:
                shifted = jnp.where(row >= step, pltpu.roll(z, step, 0), 0.0)
            else:
                shifted = jnp.where(row < n_chunks - step, pltpu.roll(z, n_chunks - step, 0), 0.0)
            z = z + cmul(shifted, a_re, a_im)
            a_re, a_im = a_re * a_re - a_im * a_im, 2.0 * a_re * a_im
            step *= 2
        if d == 0:
            z = jnp.where(row >= 1, pltpu.roll(z, 1, 0), 0.0)
        else:
            z = jnp.where(row < n_chunks - 1, pltpu.roll(z, n_chunks - 1, 0), 0.0)
        hs.append(z)
    h = jnp.concatenate(hs, axis=1).astype(BF16)
    y = jnp.dot(u, m_ref[...], preferred_element_type=F32)
    y = y + jnp.dot(h, cst_ref[...], preferred_element_type=F32)
    y_ref[...] = y


def _s5_core(u4, m, bst, cst, lam):
    b, g, n_chunks, w = u4.shape
    per_group = lambda shape: pl.BlockSpec((None,) + shape, lambda gi, bi: (gi,) + (0,) * len(shape))
    return pl.pallas_call(
        functools.partial(_s5_kernel, n_chunks=n_chunks),
        grid=(g, b),
        in_specs=[pl.BlockSpec((None, None, n_chunks, w), lambda gi, bi: (bi, gi, 0, 0)),
                  per_group((w, w)), per_group((w, w)), per_group((w, w)), per_group((4, LANE))],
        out_specs=pl.BlockSpec((None, None, n_chunks, w), lambda gi, bi: (bi, gi, 0, 0)),
        out_shape=jax.ShapeDtypeStruct((b, g, n_chunks, w), F32),
        compiler_params=_cparams(2),
        name="s5_core",
    )(u4, m, bst, cst, lam)


def _s5_post_kernel(y_ref, u_ref, d_ref, w_ref, b_ref, o_ref):
    y = y_ref[...] + d_ref[...] * u_ref[...]
    g = _gelu_tanh(y)
    z = jnp.dot(g.astype(BF16), w_ref[...], preferred_element_type=F32) + b_ref[...]
    o_ref[...] = (g * _sigmoid(z)).astype(o_ref.dtype)


def _s5_post(y2d, p2d, d_skip, glu_w, glu_b):
    t, w = y2d.shape
    row = lambda i: (i, 0)
    const = lambda i: (0, 0)
    return pl.pallas_call(
        _s5_post_kernel, grid=(t // ROW_TILE,),
        in_specs=[pl.BlockSpec((ROW_TILE, w), row),
                  pl.BlockSpec((ROW_TILE, w), lambda i: (i, ODD_BLK["su"] // 2)),
                  pl.BlockSpec((1, w), const), pl.BlockSpec((w, w), const), pl.BlockSpec((1, w), const)],
        out_specs=pl.BlockSpec((ROW_TILE, w), row),
        out_shape=jax.ShapeDtypeStruct((t, w), BF16),
        compiler_params=_cparams(1),
        name="s5_glu",
    )(y2d, p2d, d_skip.reshape(1, w), glu_w.astype(BF16), glu_b.reshape(1, w))


def _s5_operators(lam_re, lam_im, log_dt, b_re, b_im, c_re, c_im):
    L = S5_CHUNK
    tau = jnp.arange(L + 1, dtype=F32)

    def disc(d):
        lr = jnp.minimum(lam_re[d].astype(F32), -1e-4)
        li = lam_im[d].astype(F32)
        dt = jnp.exp(log_dt[d].astype(F32))[:, None]
        mag = jnp.exp(lr * dt)
        ar, ai = mag * jnp.cos(li * dt), mag * jnp.sin(li * dt)
        den = lr * lr + li * li
        nr = ar - 1.0
        cr = (nr * lr + ai * li) / den
        ci = (ai * lr - nr * li) / den
        br, bi = b_re.astype(F32), b_im.astype(F32)
        bbr = cr[..., None] * br - ci[..., None] * bi
        bbi = cr[..., None] * bi + ci[..., None] * br
        pmag = jnp.exp(lr * dt * tau[:, None, None])
        pr = pmag * jnp.cos(li * dt * tau[:, None, None])
        pi = pmag * jnp.sin(li * dt * tau[:, None, None])
        return pr, pi, bbr, bbi

    cre, cim = c_re.astype(F32), c_im.astype(F32)
    ops = []
    for d in range(2):
        pr, pi, bbr, bbi = disc(d)
        sr = pr[..., None] * bbr[None] - pi[..., None] * bbi[None]
        si = pr[..., None] * bbi[None] + pi[..., None] * bbr[None]
        kt = (jnp.einsum('gqn,tgnp->tgqp', cre, sr, precision=HIGHEST)
              - jnp.einsum('gqn,tgnp->tgqp', cim, si, precision=HIGHEST))
        chr_ = jnp.einsum('gqn,tgn->tgnq', cre, pr) - jnp.einsum('gqn,tgn->tgnq', cim, pi)
        chi_ = -jnp.einsum('gqn,tgn->tgnq', cre, pi) - jnp.einsum('gqn,tgn->tgnq', cim, pr)
        ops.append((pr, pi, sr, si, kt, chr_, chi_))

    j = np.arange(L)[:, None]
    i = np.arange(L)[None, :]
    lag_f = np.clip(i - j, 0, L)
    lag_b = np.clip(j - i, 0, L)
    kf = ops[0][4][lag_f] * jnp.asarray((i >= j), F32)[..., None, None, None]
    kb = ops[1][4][lag_b] * jnp.asarray((j >= i), F32)[..., None, None, None]
    m = (kf + kb).transpose(2, 0, 4, 1, 3).reshape(S5_GROUPS, L * S5_GROUP_CH, L * S5_GROUP_CH)

    lag_sf = (L - 1) - np.arange(L)
    lag_sb = np.arange(L)
    bst_parts = [ops[0][2][lag_sf], ops[0][3][lag_sf], ops[1][2][lag_sb], ops[1][3][lag_sb]]
    bst = jnp.stack(bst_parts, 0).transpose(2, 1, 4, 0, 3)
    bst = bst.reshape(S5_GROUPS, L * S5_GROUP_CH, 4 * S5_STATE)

    lag_cf = np.arange(L) + 1
    lag_cb = L - np.arange(L)
    cst_parts = [ops[0][5][lag_cf], ops[0][6][lag_cf], ops[1][5][lag_cb], ops[1][6][lag_cb]]
    cst = jnp.stack(cst_parts, 0).transpose(2, 0, 3, 1, 4)
    cst = cst.reshape(S5_GROUPS, 4 * S5_STATE, L * S5_GROUP_CH)

    lam_rows = []
    for d in range(2):
        pr_l, pi_l = ops[d][0][L], ops[d][1][L]
        lam_rows += [jnp.concatenate([pr_l, pr_l], -1), jnp.concatenate([pi_l, pi_l], -1)]
    lam = jnp.stack(lam_rows, 1)
    return m.astype(BF16), bst.astype(BF16), cst.astype(BF16), lam


def _out_proj_kernel(x_ref, ya_ref, yb_ref, wa_ref, wb_ref, o_ref):
    acc = jnp.dot(ya_ref[...], wa_ref[...], preferred_element_type=F32)
    acc = acc + jnp.dot(yb_ref[...], wb_ref[...], preferred_element_type=F32)
    o_ref[...] = x_ref[...] + acc


def _out_proj(x2d, ya, yb, wa, wb):
    t, d = x2d.shape
    row = lambda i: (i, 0)
    const = lambda i: (0, 0)
    return pl.pallas_call(
        _out_proj_kernel, grid=(t // ROW_TILE,),
        in_specs=[pl.BlockSpec((ROW_TILE, d), row),
                  pl.BlockSpec((ROW_TILE, ya.shape[1]), row), pl.BlockSpec((ROW_TILE, yb.shape[1]), row),
                  pl.BlockSpec(wa.shape, const), pl.BlockSpec(wb.shape, const)],
        out_specs=pl.BlockSpec((ROW_TILE, d), row),
        out_shape=jax.ShapeDtypeStruct((t, d), F32),
        compiler_params=_cparams(1),
        name="out_proj",
    )(x2d, ya, yb, wa, wb)


def _ffn_kernel(x_ref, xp_ref, xn_ref, g_ref, wup_ref, cw_ref, cb_ref, wdn_ref, gfin_ref,
                o_ref, acc_ref, *, tiles_per_seq, final_norm):
    tm = x_ref.shape[0]
    i = pl.program_id(0)
    first = (i % tiles_per_seq) == 0
    last = (i % tiles_per_seq) == tiles_per_seq - 1
    x = x_ref[...]
    g = g_ref[...]
    hp = jnp.where(first, 0.0, _rms(xp_ref[...], g))
    hn = jnp.where(last, 0.0, _rms(xn_ref[...], g))
    h = jnp.concatenate([hp, _rms(x, g), hn], axis=0).astype(BF16)
    ext = tm + 2 * SUBLANE
    acc_ref[...] = jnp.zeros_like(acc_ref)
    n_chunks = FFN_DIM // FFN_CHUNK
    for j in range(n_chunks):
        gated = []
        for part in range(2):
            u = jnp.dot(h, wup_ref[part, j], preferred_element_type=F32)
            cw = cw_ref[part, j]
            u_prev = pltpu.roll(u, 1, 0)[SUBLANE:SUBLANE + tm, :]
            u_next = pltpu.roll(u, ext - 1, 0)[SUBLANE:SUBLANE + tm, :]
            c = cb_ref[part, j] + u_prev * cw[0:1, :]
            c = c + u[SUBLANE:SUBLANE + tm, :] * cw[1:2, :]
            c = c + u_next * cw[2:3, :]
            gated.append(c)
        act = (_silu(gated[0]) * gated[1]).astype(BF16)
        acc_ref[...] += jnp.dot(act, wdn_ref[j], preferred_element_type=F32)
    out = x + acc_ref[...]
    if final_norm:
        out = _rms(out, gfin_ref[...])
    o_ref[...] = out


def _conv_ffn(x2d, seq, g, w_up, conv_w, conv_b, w_down, g_final, final_norm):
    t, d = x2d.shape
    nc = FFN_DIM // FFN_CHUNK
    tm = min(ROW_TILE, seq)
    tiles_per_seq = seq // tm
    blocks_per_tile = tm // SUBLANE
    n_row_blocks = t // SUBLANE
    wup = w_up.reshape(d, 2, nc, FFN_CHUNK).transpose(1, 2, 0, 3).astype(BF16)
    cw = conv_w.reshape(CONV_WIDTH, 2, nc, FFN_CHUNK).transpose(1, 2, 0, 3)
    cb = conv_b.reshape(2, nc, 1, FFN_CHUNK)
    wdn = w_down.reshape(nc, FFN_CHUNK, d).astype(BF16)
    full = lambda a: pl.BlockSpec(a.shape, lambda i: (0,) * a.ndim)
    kern = functools.partial(_ffn_kernel, tiles_per_seq=tiles_per_seq, final_norm=final_norm)
    gv = g.reshape(1, d)
    gf = g_final.reshape(1, d)
    return pl.pallas_call(
        kern, grid=(t // tm,),
        in_specs=[pl.BlockSpec((tm, d), lambda i: (i, 0)),
                  pl.BlockSpec((SUBLANE, d), lambda i: (jnp.maximum(i * blocks_per_tile - 1, 0), 0)),
                  pl.BlockSpec((SUBLANE, d),
                               lambda i: (jnp.minimum((i + 1) * blocks_per_tile, n_row_blocks - 1), 0)),
                  full(gv), full(wup), full(cw), full(cb), full(wdn), full(gf)],
        out_specs=pl.BlockSpec((tm, d), lambda i: (i, 0)),
        out_shape=jax.ShapeDtypeStruct((t, d), F32),
        scratch_shapes=[pltpu.VMEM((tm, d), F32)],
        compiler_params=_cparams(1),
        name="conv_ffn",
    )(x2d, x2d, x2d, gv, wup, cw, cb, wdn, gf)


def _pad_cols(w, width):
    return jnp.pad(w, ((0, 0), (0, width - w.shape[1])))


def _even_layer_mix(x2d, b, seq, norm_g, w_in, w_out, wa2, ba, gla_g, lb_f, lb_b, hgrn_g):
    sizes = (256, 256, 512, 512, 16, 16, 256, 256, 256, 512, 512)
    offs = np.concatenate([[0], np.cumsum(sizes)])
    gq, gk, gv, gr, glf, glb, hq, hzf, hzb, hi, hg = (slice(offs[i], offs[i + 1]) for i in range(11))
    w_perm = jnp.concatenate([w_in[:, s] for s in (gq, gk, gv, gr, hq, hzf, hzb, hi, hg, glf, glb)], axis=1)
    w_perm = _pad_cols(w_perm, EVEN_COLS_PADDED).astype(BF16)
    p = _norm_matmul(x2d, norm_g, w_perm).reshape(b, seq, EVEN_COLS_PADDED)

    def gate_w(direction):
        w = jnp.zeros((LANE, GLA_HEADS * GLA_DK), F32)
        w = w.at[direction * GLA_RANK:(direction + 1) * GLA_RANK, :].set(wa2[direction].astype(F32))
        return w.reshape(LANE, 2, LANE).transpose(1, 0, 2)

    baf = ba[0].astype(F32).reshape(2, 1, LANE)
    bab = ba[1].astype(F32).reshape(2, 1, LANE)
    ya = _gla_mixer(p, gate_w(0), gate_w(1), baf, bab, gla_g.astype(F32).reshape(2, 1, 2 * LANE))
    yb = _hgrn_mixer(p, lb_f.reshape(2, 1, LANE), lb_b.reshape(2, 1, LANE),
                     hgrn_g.astype(F32).reshape(2, 1, 2 * LANE))
    t = b * seq
    n_a = GLA_HEADS * GLA_DV
    return _out_proj(x2d, ya.reshape(t, n_a), yb.reshape(t, -1),
                     w_out[:n_a].astype(BF16), w_out[n_a:].astype(BF16))


def _pad_heads(w, axis):
    shape = list(w.shape)
    shape[axis:axis + 1] = [RET_HEADS, RET_DV]
    w = w.reshape(shape)
    pad = [(0, 0)] * w.ndim
    pad[axis + 1] = (0, RET_DV_PAD - RET_DV)
    w = jnp.pad(w, pad)
    shape[axis:axis + 2] = [RET_HEADS * RET_DV_PAD]
    return w.reshape(shape)


def _odd_layer_mix(x2d, b, seq, norm_g, w_in, w_out, ret_g, lam_re, lam_im, log_dt, b_re, b_im,
                   c_re, c_im, d_skip, glu_w, glu_b, rope, log_gamma):
    hk = RET_HEADS * RET_DK
    hv = RET_HEADS * RET_DV
    w_perm = jnp.concatenate([w_in[:, :2 * hk],
                              _pad_heads(w_in[:, 2 * hk:2 * hk + hv], 1),
                              _pad_heads(w_in[:, 2 * hk + hv:2 * hk + 2 * hv], 1),
                              w_in[:, 2 * hk + 2 * hv:]], axis=1).astype(BF16)
    p2d = _norm_matmul(x2d, norm_g, w_perm)
    p = p2d.reshape(b, seq, ODD_COLS_PADDED)
    g_pad = _pad_heads(ret_g.astype(F32), 0).reshape(RET_HEADS, 1, RET_DV_PAD)
    yc = _retention_mixer(p, rope[0], rope[1], log_gamma, g_pad)

    n_chunks = seq // S5_CHUNK
    su = p[:, :, ODD_BLK["su"] * LANE:]
    u4 = su.reshape(b, n_chunks, S5_CHUNK, S5_GROUPS, S5_GROUP_CH).transpose(0, 3, 1, 2, 4)
    u4 = u4.reshape(b, S5_GROUPS, n_chunks, S5_CHUNK * S5_GROUP_CH)
    m, bst, cst, lam = _s5_operators(lam_re, lam_im, log_dt, b_re, b_im, c_re, c_im)
    y4 = _s5_core(u4, m, bst, cst, lam)
    y2d = y4.reshape(b, S5_GROUPS, n_chunks, S5_CHUNK, S5_GROUP_CH).transpose(0, 2, 3, 1, 4)
    y2d = y2d.reshape(b * seq, S5_WIDTH)
    yd = _s5_post(y2d, p2d, d_skip.astype(F32), glu_w, glu_b.astype(F32))

    t = b * seq
    w_ret = _pad_heads(w_out[:hv], 0).astype(BF16)
    return _out_proj(x2d, yc.reshape(t, -1), yd, w_ret, w_out[hv:].astype(BF16))


def _rope_tables(seq):
    half = RET_DK // 2
    inv = ROPE_BASE ** (-jnp.arange(half, dtype=F32) / half)
    ang = jnp.arange(seq, dtype=F32)[:, None] * inv[None, :]
    cos, sin = jnp.cos(ang), jnp.sin(ang)
    return jnp.concatenate([cos, cos], -1), jnp.concatenate([-sin, sin], -1)


def _retention_log_gamma(chunk):
    hidx = jnp.arange(RET_HEADS, dtype=F32)
    lg = jnp.stack([jnp.log1p(-jnp.exp2(-5.0 - hidx)), jnp.log1p(-jnp.exp2(-5.5 - hidx))], axis=1)
    return jnp.broadcast_to(lg[:, :, None], (RET_HEADS, 2, chunk))


def _hgrn_lower_bounds(lb_logits):
    p = jax.nn.softmax(lb_logits.astype(F32), axis=1)
    return jnp.cumsum(p, axis=1) - p[:, :1]


def kernel(x, mix_norm_g, ffn_norm_g, final_norm_g, w_in_even, w_out_even, gla_wa2, gla_ba, gla_norm_g,
           hgrn_lb_logits, hgrn_norm_g, w_in_odd, w_out_odd, ret_norm_g, s5_lam_re, s5_lam_im, s5_log_dt,
           s5_b_re, s5_b_im, s5_c_re, s5_c_im, s5_d, s5_glu_w, s5_glu_b,
           ffn_w_up, ffn_conv_w, ffn_conv_b, ffn_w_down):
    b, seq, d = x.shape
    lbs = _hgrn_lower_bounds(hgrn_lb_logits)
    rope = _rope_tables(seq)
    log_gamma = _retention_log_gamma(min(RET_CHUNK, seq))
    x2d = x.reshape(b * seq, d)
    for layer in range(DEPTH):
        j = layer // 2
        if layer % 2 == 0:
            x2d = _even_layer_mix(x2d, b, seq, mix_norm_g[layer], w_in_even[j], w_out_even[j], gla_wa2[j],
                                  gla_ba[j], gla_norm_g[j], lbs[0, j], lbs[1, j], hgrn_norm_g[j])
        else:
            x2d = _odd_layer_mix(x2d, b, seq, mix_norm_g[layer], w_in_odd[j], w_out_odd[j], ret_norm_g[j],
                                 s5_lam_re[j], s5_lam_im[j], s5_log_dt[j], s5_b_re[j], s5_b_im[j],
                                 s5_c_re[j], s5_c_im[j], s5_d[j], s5_glu_w[j], s5_glu_b[j], rope, log_gamma)
        x2d = _conv_ffn(x2d, seq, ffn_norm_g[layer], ffn_w_up[layer], ffn_conv_w[layer], ffn_conv_b[layer],
                        ffn_w_down[layer], final_norm_g, final_norm=(layer == DEPTH - 1))
    return x2d.reshape(b, seq, d)
```

```python
import functools
import math

import jax
import jax.numpy as jnp
import numpy as np
from jax import lax
from jax.experimental import pallas as pl
from jax.experimental.pallas import tpu as pltpu

F32 = jnp.float32
BF16 = jnp.bfloat16
HIGHEST = lax.Precision.HIGHEST

D_MODEL = 1024
DEPTH = 4
EPS = 1e-6

GLA_HEADS = 4
GLA_DK = 64
GLA_DV = 128
GLA_RANK = 16
GLA_GATE_NORM = 16.0
HGRN_HEADS = 4
HGRN_DK = 64
HGRN_DV = 128
HGRN_MIN_F = 1e-20

RET_HEADS = 4
RET_DK = 128
RET_DV = 192
RET_DV_PAD = 256
ROPE_BASE = 10000.0

S5_WIDTH = 256
S5_GROUP_CH = 16
S5_GROUPS = 16
S5_STATE = 64
S5_CHUNK = 16

FFN_DIM = 2816
FFN_CHUNK = 256
FFN_GROUP = 4 * FFN_CHUNK
CONV_WIDTH = 3

LANE = 128
SUBLANE = 8
GATED_CHUNK = 64
RET_CHUNK = 256
ROW_TILE = 512
VMEM_LIMIT = 56 * 1024 * 1024

NT_DIMS = (((1,), (1,)), ((), ()))
TN_DIMS = (((0,), (0,)), ((), ()))


def _cparams(n_axes):
    return pltpu.CompilerParams(dimension_semantics=("arbitrary",) * n_axes,
                                vmem_limit_bytes=VMEM_LIMIT)


def _rms(x, g):
    return x * lax.rsqrt(jnp.mean(x * x, axis=-1, keepdims=True) + EPS) * g


def _sigmoid(x):
    return 1.0 / (1.0 + jnp.exp(-x))


def _silu(x):
    return x * _sigmoid(x)


def _log_sigmoid(z):
    return jnp.minimum(z, 0.0) - jnp.log(1.0 + jnp.exp(-jnp.abs(z)))


def _gelu_tanh(x):
    c = math.sqrt(2.0 / math.pi)
    return 0.5 * x * (1.0 + jnp.tanh(c * (x + 0.044715 * (x * x * x))))


def _norm_matmul_kernel(x_ref, g_ref, w_ref, o_ref, *, col_chunk):
    h = _rms(x_ref[...], g_ref[...]).astype(BF16)
    n_out = o_ref.shape[1]
    for j in range(n_out // col_chunk):
        cols = slice(j * col_chunk, (j + 1) * col_chunk)
        o_ref[:, cols] = jnp.dot(h, w_ref[:, cols], preferred_element_type=F32)


def _norm_matmul(x2d, g, w):
    t, d = x2d.shape
    e = w.shape[1]
    col_chunk = next(c * LANE for c in (4, 3, 2, 1) if e % (c * LANE) == 0)
    return pl.pallas_call(
        functools.partial(_norm_matmul_kernel, col_chunk=col_chunk),
        grid=(t // ROW_TILE,),
        in_specs=[pl.BlockSpec((ROW_TILE, d), lambda i: (i, 0)),
                  pl.BlockSpec((1, d), lambda i: (0, 0)),
                  pl.BlockSpec((d, e), lambda i: (0, 0))],
        out_specs=pl.BlockSpec((ROW_TILE, e), lambda i: (i, 0)),
        out_shape=jax.ShapeDtypeStruct((t, e), F32),
        compiler_params=_cparams(1),
        name="norm_in_proj",
    )(x2d, g.reshape(1, d), w)


def _chunk_cumsum(x, chunk, forward):
    n_rows = x.shape[0]
    pos = lax.broadcasted_iota(jnp.int32, x.shape, 0) % chunk
    step = 1
    while step < chunk:
        if forward:
            x = x + jnp.where(pos >= step, pltpu.roll(x, step, 0), 0.0)
        else:
            x = x + jnp.where(pos < chunk - step, pltpu.roll(x, n_rows - step, 0), 0.0)
        step *= 2
    return x


def _chunk_row(x, chunk, r, rows_out):
    n = x.shape[0] // chunk
    x3 = x.reshape(n, chunk, x.shape[1])[:, r:r + 1, :]
    return jnp.broadcast_to(x3, (n, rows_out, x.shape[1])).reshape(n * rows_out, x.shape[1])


def _gated_prepare(q, k, la, bufs, *, chunk, forward):
    qt_ref, kt_ref, qh_ref, kh_ref, dec_ref = bufs
    cum = _chunk_cumsum(la, chunk, forward)
    ref = _chunk_row(cum, chunk, chunk // 2, chunk)
    tot_row = chunk - 1 if forward else 0
    tot = _chunk_row(cum, chunk, tot_row, chunk)
    qt_ref[...] = (q * jnp.exp(cum - ref)).astype(BF16)
    kt_ref[...] = (k * jnp.exp(ref - cum)).astype(BF16)
    qh_ref[...] = (q * jnp.exp(cum)).astype(BF16)
    kh_ref[...] = (k * jnp.exp(tot - cum)).astype(BF16)
    dec_ref[...] = jnp.exp(_chunk_row(cum, chunk, tot_row, SUBLANE))


def _gated_scan(bufs_f, bufs_b, vb_ref, o_f, o_b, st_f, st_b, *, seq, chunk):
    n_chunks = seq // chunk
    row = lax.broadcasted_iota(jnp.int32, (2 * chunk, chunk), 0) % chunk
    col = lax.broadcasted_iota(jnp.int32, (2 * chunk, chunk), 1)
    lower = row >= col
    upper = row <= col
    head0_lane = lax.broadcasted_iota(jnp.int32, (chunk, LANE), 1) < GLA_DK
    head0_val = lax.broadcasted_iota(jnp.int32, (chunk, 2 * LANE), 1) < LANE
    st_row = lax.broadcasted_iota(jnp.int32, (2 * LANE, LANE), 0) < LANE
    st_col = lax.broadcasted_iota(jnp.int32, (2 * LANE, LANE), 1) < GLA_DK
    st_mask = (st_row == st_col).astype(F32)

    st_f[...] = jnp.zeros_like(st_f)
    st_b[...] = jnp.zeros_like(st_b)

    def visit(c, forward):
        qt_ref, kt_ref, qh_ref, kh_ref, dec_ref = bufs_f if forward else bufs_b
        st_ref = st_f if forward else st_b
        o_ref = o_f if forward else o_b
        rows = pl.ds(pl.multiple_of(c * chunk, chunk), chunk)
        qt = qt_ref[rows, :]
        zero = jnp.zeros_like(qt)
        q2 = jnp.concatenate([jnp.where(head0_lane, qt, zero), jnp.where(head0_lane, zero, qt)], axis=0)
        a = lax.dot_general(q2, kt_ref[rows, :], NT_DIMS, preferred_element_type=F32)
        a = jnp.where(lower if forward else upper, a, 0.0).astype(BF16)
        vb = vb_ref[rows, :]
        o2 = jnp.dot(a, vb, preferred_element_type=F32)
        o = jnp.where(head0_val, o2[:chunk, :], o2[chunk:, :])
        st = st_ref[...]
        o = o + lax.dot_general(qh_ref[rows, :], st.astype(BF16), NT_DIMS, preferred_element_type=F32)
        ds = lax.dot_general(vb, kh_ref[rows, :], TN_DIMS, preferred_element_type=F32)
        dec = dec_ref[pl.ds(pl.multiple_of(c * SUBLANE, SUBLANE), SUBLANE), :][0:1, :]
        st_ref[...] = st * dec + ds * st_mask
        o_ref[rows, :] = o

    def body(c, carry):
        visit(c, True)
        visit(n_chunks - 1 - c, False)
        return carry

    lax.fori_loop(0, n_chunks, body, 0, unroll=4)


def _head_rmsnorm_gate(o_f, o_b, gate_ref, g_ref, y_ref):
    o = o_f[...] + o_b[...]
    gate = gate_ref[...]
    g = g_ref[...]
    outs = []
    for h in range(2):
        cols = slice(h * LANE, (h + 1) * LANE)
        outs.append(_rms(o[:, cols], g[:, cols]) * _silu(gate[:, cols]))
    y_ref[...] = jnp.concatenate(outs, axis=1).astype(y_ref.dtype)


def _gla_pair_kernel(q_ref, k_ref, v_ref, r_ref, lr_ref, wgf_ref, wgb_ref, baf_ref, bab_ref, g_ref,
                     y_ref, *scratch, seq, chunk):
    bufs_f, bufs_b, (vb_ref, o_f, o_b, st_f, st_b) = scratch[0:5], scratch[5:10], scratch[10:]
    q = q_ref[...]
    k = k_ref[...] * (GLA_DK ** -0.5)
    lr = lr_ref[...]
    for w_ref, b_ref, bufs, forward in ((wgf_ref, baf_ref, bufs_f, True), (wgb_ref, bab_ref, bufs_b, False)):
        z = jnp.dot(lr, w_ref[...], precision=HIGHEST, preferred_element_type=F32) + b_ref[...]
        la = _log_sigmoid(z) * (1.0 / GLA_GATE_NORM)
        _gated_prepare(q, k, la, bufs, chunk=chunk, forward=forward)
    vb_ref[...] = v_ref[...].astype(BF16)
    _gated_scan(bufs_f, bufs_b, vb_ref, o_f, o_b, st_f, st_b, seq=seq, chunk=chunk)
    _head_rmsnorm_gate(o_f, o_b, r_ref, g_ref, y_ref)


def _hgrn_pair_kernel(q_ref, zf_ref, zb_ref, v_ref, gate_ref, lbf_ref, lbb_ref, g_ref,
                      y_ref, *scratch, seq, chunk):
    bufs_f, bufs_b, (vb_ref, o_f, o_b, st_f, st_b) = scratch[0:5], scratch[5:10], scratch[10:]
    q = _silu(q_ref[...])
    for z_ref, lb_ref, bufs, forward in ((zf_ref, lbf_ref, bufs_f, True), (zb_ref, lbb_ref, bufs_b, False)):
        z = z_ref[...]
        lb = lb_ref[...]
        f = lb + (1.0 - lb) * _sigmoid(z)
        la = jnp.log(jnp.maximum(f, HGRN_MIN_F))
        k = (1.0 - lb) * _sigmoid(-z)
        _gated_prepare(q, k, la, bufs, chunk=chunk, forward=forward)
    vb_ref[...] = v_ref[...].astype(BF16)
    _gated_scan(bufs_f, bufs_b, vb_ref, o_f, o_b, st_f, st_b, seq=seq, chunk=chunk)
    _head_rmsnorm_gate(o_f, o_b, gate_ref, g_ref, y_ref)


def _col_spec(seq, width, block_fn):
    return pl.BlockSpec((None, seq, width), lambda b, p: (b, 0, block_fn(p)))


def _pair_spec(shape):
    return pl.BlockSpec((None,) + shape, lambda b, p: (p,) + (0,) * len(shape))


def _gated_scratch(seq, chunk):
    per_dir = [pltpu.VMEM((seq, LANE), BF16) for _ in range(4)]
    per_dir.append(pltpu.VMEM((seq // chunk * SUBLANE, LANE), F32))
    return per_dir * 2 + [pltpu.VMEM((seq, 2 * LANE), BF16),
                          pltpu.VMEM((seq, 2 * LANE), F32), pltpu.VMEM((seq, 2 * LANE), F32),
                          pltpu.VMEM((2 * LANE, LANE), F32), pltpu.VMEM((2 * LANE, LANE), F32)]


EVEN_BLK = dict(gq=0, gk=2, gv=4, gr=8, hq=12, hzf=14, hzb=16, hi=18, hg=22, lr=26)
EVEN_COLS_PADDED = 27 * LANE


def _gla_mixer(p3, wgf, wgb, baf, bab, norm_g):
    b, seq, _ = p3.shape
    blk = EVEN_BLK
    kern = functools.partial(_gla_pair_kernel, seq=seq, chunk=GATED_CHUNK)
    return pl.pallas_call(
        kern, grid=(b, 2),
        in_specs=[_col_spec(seq, LANE, lambda p: blk["gq"] + p),
                  _col_spec(seq, LANE, lambda p: blk["gk"] + p),
                  _col_spec(seq, 2 * LANE, lambda p: blk["gv"] // 2 + p),
                  _col_spec(seq, 2 * LANE, lambda p: blk["gr"] // 2 + p),
                  _col_spec(seq, LANE, lambda p: blk["lr"]),
                  _pair_spec((LANE, LANE)), _pair_spec((LANE, LANE)),
                  _pair_spec((1, LANE)), _pair_spec((1, LANE)), _pair_spec((1, 2 * LANE))],
        out_specs=pl.BlockSpec((None, seq, 2 * LANE), lambda b, p: (b, 0, p)),
        out_shape=jax.ShapeDtypeStruct((b, seq, GLA_HEADS * GLA_DV), BF16),
        scratch_shapes=_gated_scratch(seq, GATED_CHUNK),
        compiler_params=_cparams(2),
        name="gla_mixer",
    )(p3, p3, p3, p3, p3, wgf, wgb, baf, bab, norm_g)


def _hgrn_mixer(p3, lbf, lbb, norm_g):
    b, seq, _ = p3.shape
    blk = EVEN_BLK
    kern = functools.partial(_hgrn_pair_kernel, seq=seq, chunk=GATED_CHUNK)
    return pl.pallas_call(
        kern, grid=(b, 2),
        in_specs=[_col_spec(seq, LANE, lambda p: blk["hq"] + p),
                  _col_spec(seq, LANE, lambda p: blk["hzf"] + p),
                  _col_spec(seq, LANE, lambda p: blk["hzb"] + p),
                  _col_spec(seq, 2 * LANE, lambda p: blk["hi"] // 2 + p),
                  _col_spec(seq, 2 * LANE, lambda p: blk["hg"] // 2 + p),
                  _pair_spec((1, LANE)), _pair_spec((1, LANE)), _pair_spec((1, 2 * LANE))],
        out_specs=pl.BlockSpec((None, seq, 2 * LANE), lambda b, p: (b, 0, p)),
        out_shape=jax.ShapeDtypeStruct((b, seq, HGRN_HEADS * HGRN_DV), BF16),
        scratch_shapes=_gated_scratch(seq, GATED_CHUNK),
        compiler_params=_cparams(2),
        name="hgrn2_mixer",
    )(p3, p3, p3, p3, p3, lbf, lbb, norm_g)


def _retention_kernel(q_ref, k_ref, v_ref, gate_ref, cos_ref, sin_ref, lg_ref, g_ref,
                      y_ref, qr, kr, o_acc, s_f, s_b, *, seq, chunk):
    half = RET_DK // 2
    cos = cos_ref[...]
    sin = sin_ref[...]
    q = q_ref[...]
    k = k_ref[...]
    qr[...] = q * cos + pltpu.roll(q, half, 1) * sin
    kr[...] = (k * cos + pltpu.roll(k, half, 1) * sin) * (RET_DK ** -0.5)

    lg_f = lg_ref[0:1, :]
    lg_b = lg_ref[1:2, :]
    n_chunks = seq // chunk
    row = lax.broadcasted_iota(jnp.int32, (chunk, chunk), 0)
    col = lax.broadcasted_iota(jnp.int32, (chunk, chunk), 1)
    diff = (row - col).astype(F32)
    dmat = (jnp.where(row >= col, jnp.exp(lg_f * diff), 0.0)
            + jnp.where(row <= col, jnp.exp(-lg_b * diff), 0.0))
    pos = lax.broadcasted_iota(jnp.int32, (chunk, LANE), 0).astype(F32)
    lgf = lg_f[:, 0:LANE]
    lgb = lg_b[:, 0:LANE]
    qdec_f = jnp.exp(lgf * (pos + 1.0))
    kdec_f = jnp.exp(lgf * (chunk - 1.0 - pos))
    qdec_b = jnp.exp(lgb * (chunk - pos))
    kdec_b = jnp.exp(lgb * pos)
    tot_f = jnp.exp(lgf[:, 0:1] * float(chunk))
    tot_b = jnp.exp(lgb[:, 0:1] * float(chunk))

    s_f[...] = jnp.zeros_like(s_f)
    s_b[...] = jnp.zeros_like(s_b)

    def body(c, carry):
        rows = pl.ds(pl.multiple_of(c * chunk, chunk), chunk)
        qc = qr[rows, :]
        kc = kr[rows, :]
        vc = v_ref[rows, :].astype(BF16)
        a = lax.dot_general(qc.astype(BF16), kc.astype(BF16), NT_DIMS, preferred_element_type=F32)
        o = jnp.dot((a * dmat).astype(BF16), vc, preferred_element_type=F32)
        s_old = s_f[...]
        o = o + jnp.dot((qc * qdec_f).astype(BF16), s_old.astype(BF16), preferred_element_type=F32)
        s_f[...] = s_old * tot_f + lax.dot_general((kc * kdec_f).astype(BF16), vc, TN_DIMS,
                                                   preferred_element_type=F32)
        o_acc[rows, :] = o
        return carry

    def body_b(c, carry):
        rows = pl.ds(pl.multiple_of((n_chunks - 1 - c) * chunk, chunk), chunk)
        qc = qr[rows, :]
        kc = kr[rows, :]
        vc = v_ref[rows, :].astype(BF16)
        s_old = s_b[...]
        o = jnp.dot((qc * qdec_b).astype(BF16), s_old.astype(BF16), preferred_element_type=F32)
        s_b[...] = s_old * tot_b + lax.dot_general((kc * kdec_b).astype(BF16), vc, TN_DIMS,
                                                   preferred_element_type=F32)
        o_acc[rows, :] = o_acc[rows, :] + o
        return carry

    lax.fori_loop(0, n_chunks, body, 0)
    lax.fori_loop(0, n_chunks, body_b, 0)

    o = o_acc[...]
    vlane = lax.broadcasted_iota(jnp.int32, (1, RET_DV_PAD), 1) < RET_DV
    mu = jnp.sum(o, axis=-1, keepdims=True) * (1.0 / RET_DV)
    cen = jnp.where(vlane, o - mu, 0.0)
    var = jnp.sum(cen * cen, axis=-1, keepdims=True) * (1.0 / RET_DV)
    y = cen * lax.rsqrt(var + EPS) * g_ref[...]
    y_ref[...] = (y * _silu(gate_ref[...])).astype(y_ref.dtype)


ODD_BLK = dict(rq=0, rk=4, rv=8, rg=16, su=24)
ODD_COLS_PADDED = 26 * LANE


def _retention_mixer(p3, cos2, sin2, log_gamma, norm_g):
    b, seq, _ = p3.shape
    blk = ODD_BLK
    chunk = min(RET_CHUNK, seq)
    kern = functools.partial(_retention_kernel, seq=seq, chunk=chunk)
    head = lambda shape: pl.BlockSpec((None,) + shape, lambda b, h: (h,) + (0,) * len(shape))
    table = pl.BlockSpec((seq, LANE), lambda b, h: (0, 0))
    return pl.pallas_call(
        kern, grid=(b, RET_HEADS),
        in_specs=[pl.BlockSpec((None, seq, LANE), lambda b, h: (b, 0, blk["rq"] + h)),
                  pl.BlockSpec((None, seq, LANE), lambda b, h: (b, 0, blk["rk"] + h)),
                  pl.BlockSpec((None, seq, RET_DV_PAD), lambda b, h: (b, 0, blk["rv"] // 2 + h)),
                  pl.BlockSpec((None, seq, RET_DV_PAD), lambda b, h: (b, 0, blk["rg"] // 2 + h)),
                  table, table, head((2, chunk)), head((1, RET_DV_PAD))],
        out_specs=pl.BlockSpec((None, seq, RET_DV_PAD), lambda b, h: (b, 0, h)),
        out_shape=jax.ShapeDtypeStruct((b, seq, RET_HEADS * RET_DV_PAD), BF16),
        scratch_shapes=[pltpu.VMEM((seq, LANE), F32), pltpu.VMEM((seq, LANE), F32),
                        pltpu.VMEM((seq, RET_DV_PAD), F32),
                        pltpu.VMEM((RET_DK, RET_DV_PAD), F32), pltpu.VMEM((RET_DK, RET_DV_PAD), F32)],
        compiler_params=_cparams(2),
        name="retention_mixer",
    )(p3, p3, p3, p3, cos2, sin2, log_gamma, norm_g)


def _s5_kernel(u_ref, m_ref, bst_ref, cst_ref, lam_ref, y_ref, *, n_chunks):
    u = u_ref[...].astype(BF16)
    x = jnp.dot(u, bst_ref[...], preferred_element_type=F32)
    half = S5_STATE
    row = lax.broadcasted_iota(jnp.int32, (n_chunks, LANE), 0)
    lane = lax.broadcasted_iota(jnp.int32, (1, LANE), 1)
    sign = jnp.where(lane < half, -1.0, 1.0)

    def cmul(z, a_re, a_im_signed):
        return z * a_re + pltpu.roll(z, half, 1) * a_im_signed

    hs = []
    for d in range(2):
        z = x[:, d * LANE:(d + 1) * LANE]
        a_re = lam_ref[2 * d:2 * d + 1, :]
        a_im = lam_ref[2 * d + 1:2 * d + 2, :] * sign
        step = 1
        while step < n_chunks:
            if d == 0:
                shifted = jnp.where(row >= step, pltpu.roll(z, step, 0), 0.0)
            else:
                shifted = jnp.where(row < n_chunks - step, pltpu.roll(z, n_chunks - step, 0), 0.0)
            z = z + cmul(shifted, a_re, a_im)
            a_re, a_im = a_re * a_re - a_im * a_im, 2.0 * a_re * a_im
            step *= 2
        if d == 0:
            z = jnp.where(row >= 1, pltpu.roll(z, 1, 0), 0.0)
        else:
            z = jnp.where(row < n_chunks - 1, pltpu.roll(z, n_chunks - 1, 0), 0.0)
        hs.append(z)
    h = jnp.concatenate(hs, axis=1).astype(BF16)
    y = jnp.dot(u, m_ref[...], preferred_element_type=F32)
    y = y + jnp.dot(h, cst_ref[...], preferred_element_type=F32)
    y_ref[...] = y


def _s5_core(u4, m, bst, cst, lam):
    b, g, n_chunks, w = u4.shape
    per_group = lambda shape: pl.BlockSpec((None,) + shape, lambda gi, bi: (gi,) + (0,) * len(shape))
    return pl.pallas_call(
        functools.partial(_s5_kernel, n_chunks=n_chunks),
        grid=(g, b),
        in_specs=[pl.BlockSpec((None, None, n_chunks, w), lambda gi, bi: (bi, gi, 0, 0)),
                  per_group((w, w)), per_group((w, w)), per_group((w, w)), per_group((4, LANE))],
        out_specs=pl.BlockSpec((None, None, n_chunks, w), lambda gi, bi: (bi, gi, 0, 0)),
        out_shape=jax.ShapeDtypeStruct((b, g, n_chunks, w), F32),
        compiler_params=_cparams(2),
        name="s5_core",
    )(u4, m, bst, cst, lam)


def _s5_post_kernel(y_ref, u_ref, d_ref, w_ref, b_ref, o_ref):
    y = y_ref[...] + d_ref[...] * u_ref[...]
    g = _gelu_tanh(y)
    z = jnp.dot(g.astype(BF16), w_ref[...], preferred_element_type=F32) + b_ref[...]
    o_ref[...] = (g * _sigmoid(z)).astype(o_ref.dtype)


def _s5_post(y2d, p2d, d_skip, glu_w, glu_b):
    t, w = y2d.shape
    row = lambda i: (i, 0)
    const = lambda i: (0, 0)
    return pl.pallas_call(
        _s5_post_kernel, grid=(t // ROW_TILE,),
        in_specs=[pl.BlockSpec((ROW_TILE, w), row),
                  pl.BlockSpec((ROW_TILE, w), lambda i: (i, ODD_BLK["su"] // 2)),
                  pl.BlockSpec((1, w), const), pl.BlockSpec((w, w), const), pl.BlockSpec((1, w), const)],
        out_specs=pl.BlockSpec((ROW_TILE, w), row),
        out_shape=jax.ShapeDtypeStruct((t, w), BF16),
        compiler_params=_cparams(1),
        name="s5_glu",
    )(y2d, p2d, d_skip.reshape(1, w), glu_w.astype(BF16), glu_b.reshape(1, w))


def _s5_operators(lam_re, lam_im, log_dt, b_re, b_im, c_re, c_im):
    L = S5_CHUNK
    tau = jnp.arange(L + 1, dtype=F32)

    def disc(d):
        lr = jnp.minimum(lam_re[d].astype(F32), -1e-4)
        li = lam_im[d].astype(F32)
        dt = jnp.exp(log_dt[d].astype(F32))[:, None]
        mag = jnp.exp(lr * dt)
        ar, ai = mag * jnp.cos(li * dt), mag * jnp.sin(li * dt)
        den = lr * lr + li * li
        nr = ar - 1.0
        cr = (nr * lr + ai * li) / den
        ci = (ai * lr - nr * li) / den
        br, bi = b_re.astype(F32), b_im.astype(F32)
        bbr = cr[..., None] * br - ci[..., None] * bi
        bbi = cr[..., None] * bi + ci[..., None] * br
        pmag = jnp.exp(lr * dt * tau[:, None, None])
        pr = pmag * jnp.cos(li * dt * tau[:, None, None])
        pi = pmag * jnp.sin(li * dt * tau[:, None, None])
        return pr, pi, bbr, bbi

    cre, cim = c_re.astype(F32), c_im.astype(F32)
    ops = []
    for d in range(2):
        pr, pi, bbr, bbi = disc(d)
        sr = pr[..., None] * bbr[None] - pi[..., None] * bbi[None]
        si = pr[..., None] * bbi[None] + pi[..., None] * bbr[None]
        kt = (jnp.einsum('gqn,tgnp->tgqp', cre, sr, precision=HIGHEST)
              - jnp.einsum('gqn,tgnp->tgqp', cim, si, precision=HIGHEST))
        chr_ = jnp.einsum('gqn,tgn->tgnq', cre, pr) - jnp.einsum('gqn,tgn->tgnq', cim, pi)
        chi_ = -jnp.einsum('gqn,tgn->tgnq', cre, pi) - jnp.einsum('gqn,tgn->tgnq', cim, pr)
        ops.append((pr, pi, sr, si, kt, chr_, chi_))

    j = np.arange(L)[:, None]
    i = np.arange(L)[None, :]
    lag_f = np.clip(i - j, 0, L)
    lag_b = np.clip(j - i, 0, L)
    kf = ops[0][4][lag_f] * jnp.asarray((i >= j), F32)[..., None, None, None]
    kb = ops[1][4][lag_b] * jnp.asarray((j >= i), F32)[..., None, None, None]
    m = (kf + kb).transpose(2, 0, 4, 1, 3).reshape(S5_GROUPS, L * S5_GROUP_CH, L * S5_GROUP_CH)

    lag_sf = (L - 1) - np.arange(L)
    lag_sb = np.arange(L)
    bst_parts = [ops[0][2][lag_sf], ops[0][3][lag_sf], ops[1][2][lag_sb], ops[1][3][lag_sb]]
    bst = jnp.stack(bst_parts, 0).transpose(2, 1, 4, 0, 3)
    bst = bst.reshape(S5_GROUPS, L * S5_GROUP_CH, 4 * S5_STATE)

    lag_cf = np.arange(L) + 1
    lag_cb = L - np.arange(L)
    cst_parts = [ops[0][5][lag_cf], ops[0][6][lag_cf], ops[1][5][lag_cb], ops[1][6][lag_cb]]
    cst = jnp.stack(cst_parts, 0).transpose(2, 0, 3, 1, 4)
    cst = cst.reshape(S5_GROUPS, 4 * S5_STATE, L * S5_GROUP_CH)

    lam_rows = []
    for d in range(2):
        pr_l, pi_l = ops[d][0][L], ops[d][1][L]
        lam_rows += [jnp.concatenate([pr_l, pr_l], -1), jnp.concatenate([pi_l, pi_l], -1)]
    lam = jnp.stack(lam_rows, 1)
    return m.astype(BF16), bst.astype(BF16), cst.astype(BF16), lam


def _out_proj_kernel(x_ref, ya_ref, yb_ref, wa_ref, wb_ref, o_ref):
    acc = jnp.dot(ya_ref[...], wa_ref[...], preferred_element_type=F32)
    acc = acc + jnp.dot(yb_ref[...], wb_ref[...], preferred_element_type=F32)
    o_ref[...] = x_ref[...] + acc


def _out_proj(x2d, ya, yb, wa, wb):
    t, d = x2d.shape
    row = lambda i: (i, 0)
    const = lambda i: (0, 0)
    return pl.pallas_call(
        _out_proj_kernel, grid=(t // ROW_TILE,),
        in_specs=[pl.BlockSpec((ROW_TILE, d), row),
                  pl.BlockSpec((ROW_TILE, ya.shape[1]), row), pl.BlockSpec((ROW_TILE, yb.shape[1]), row),
                  pl.BlockSpec(wa.shape, const), pl.BlockSpec(wb.shape, const)],
        out_specs=pl.BlockSpec((ROW_TILE, d), row),
        out_shape=jax.ShapeDtypeStruct((t, d), F32),
        compiler_params=_cparams(1),
        name="out_proj",
    )(x2d, ya, yb, wa, wb)


def _ffn_kernel(x_ref, xp_ref, xn_ref, g_ref, wup_ref, cw_ref, cb_ref, wdn_ref, gfin_ref,
                o_ref, act_ref, *, tiles_per_seq, final_norm):
    tm = x_ref.shape[0]
    i = pl.program_id(0)
    first = (i % tiles_per_seq) == 0
    last = (i % tiles_per_seq) == tiles_per_seq - 1
    x = x_ref[...]
    g = g_ref[...]
    hp = jnp.where(first, 0.0, _rms(xp_ref[...], g))
    hn = jnp.where(last, 0.0, _rms(xn_ref[...], g))
    h = jnp.concatenate([hp, _rms(x, g), hn], axis=0).astype(BF16)
    ext = tm + 2 * SUBLANE
    out = x
    for j0 in range(0, FFN_DIM, FFN_GROUP):
        width = min(FFN_GROUP, FFN_DIM - j0)
        for j in range(j0, j0 + width, FFN_CHUNK):
            gated = []
            for base in (0, FFN_DIM):
                cols = slice(base + j, base + j + FFN_CHUNK)
                u = jnp.dot(h, wup_ref[:, cols], preferred_element_type=F32)
                u_prev = pltpu.roll(u, 1, 0)[SUBLANE:SUBLANE + tm, :]
                u_next = pltpu.roll(u, ext - 1, 0)[SUBLANE:SUBLANE + tm, :]
                c = cb_ref[:, cols] + u_prev * cw_ref[0:1, cols]
                c = c + u[SUBLANE:SUBLANE + tm, :] * cw_ref[1:2, cols]
                c = c + u_next * cw_ref[2:3, cols]
                gated.append(c)
            act_ref[:, j - j0:j - j0 + FFN_CHUNK] = (_silu(gated[0]) * gated[1]).astype(BF16)
        out = out + jnp.dot(act_ref[:, 0:width], wdn_ref[j0:j0 + width, :], preferred_element_type=F32)
    if final_norm:
        out = _rms(out, gfin_ref[...])
    o_ref[...] = out


def _conv_ffn(x2d, seq, g, w_up, conv_w, conv_b, w_down, g_final, final_norm):
    t, d = x2d.shape
    tm = min(ROW_TILE, seq)
    tiles_per_seq = seq // tm
    blocks_per_tile = tm // SUBLANE
    n_row_blocks = t // SUBLANE
    full = lambda a: pl.BlockSpec(a.shape, lambda i: (0,) * a.ndim)
    kern = functools.partial(_ffn_kernel, tiles_per_seq=tiles_per_seq, final_norm=final_norm)
    gv = g.reshape(1, d)
    gf = g_final.reshape(1, d)
    wup = w_up.astype(BF16)
    wdn = w_down.astype(BF16)
    cb = conv_b.reshape(1, 2 * FFN_DIM)
    return pl.pallas_call(
        kern, grid=(t // tm,),
        in_specs=[pl.BlockSpec((tm, d), lambda i: (i, 0)),
                  pl.BlockSpec((SUBLANE, d), lambda i: (jnp.maximum(i * blocks_per_tile - 1, 0), 0)),
                  pl.BlockSpec((SUBLANE, d),
                               lambda i: (jnp.minimum((i + 1) * blocks_per_tile, n_row_blocks - 1), 0)),
                  full(gv), full(wup), full(conv_w), full(cb), full(wdn), full(gf)],
        out_specs=pl.BlockSpec((tm, d), lambda i: (i, 0)),
        out_shape=jax.ShapeDtypeStruct((t, d), F32),
        scratch_shapes=[pltpu.VMEM((tm, FFN_GROUP), BF16)],
        compiler_params=_cparams(1),
        name="conv_ffn",
    )(x2d, x2d, x2d, gv, wup, conv_w, cb, wdn, gf)


def _pad_cols(w, width):
    return jnp.pad(w, ((0, 0), (0, width - w.shape[1])))


def _even_layer_mix(x2d, b, seq, norm_g, w_in, w_out, wa2, ba, gla_g, lb_f, lb_b, hgrn_g):
    sizes = (256, 256, 512, 512, 16, 16, 256, 256, 256, 512, 512)
    offs = np.concatenate([[0], np.cumsum(sizes)])
    gq, gk, gv, gr, glf, glb, hq, hzf, hzb, hi, hg = (slice(offs[i], offs[i + 1]) for i in range(11))
    w_perm = jnp.concatenate([w_in[:, s] for s in (gq, gk, gv, gr, hq, hzf, hzb, hi, hg, glf, glb)], axis=1)
    w_perm = _pad_cols(w_perm, EVEN_COLS_PADDED).astype(BF16)
    p = _norm_matmul(x2d, norm_g, w_perm).reshape(b, seq, EVEN_COLS_PADDED)

    def gate_w(direction):
        w = jnp.zeros((LANE, GLA_HEADS * GLA_DK), F32)
        w = w.at[direction * GLA_RANK:(direction + 1) * GLA_RANK, :].set(wa2[direction].astype(F32))
        return w.reshape(LANE, 2, LANE).transpose(1, 0, 2)

    baf = ba[0].astype(F32).reshape(2, 1, LANE)
    bab = ba[1].astype(F32).reshape(2, 1, LANE)
    ya = _gla_mixer(p, gate_w(0), gate_w(1), baf, bab, gla_g.astype(F32).reshape(2, 1, 2 * LANE))
    yb = _hgrn_mixer(p, lb_f.reshape(2, 1, LANE), lb_b.reshape(2, 1, LANE),
                     hgrn_g.astype(F32).reshape(2, 1, 2 * LANE))
    t = b * seq
    n_a = GLA_HEADS * GLA_DV
    return _out_proj(x2d, ya.reshape(t, n_a), yb.reshape(t, -1),
                     w_out[:n_a].astype(BF16), w_out[n_a:].astype(BF16))


def _pad_heads(w, axis):
    shape = list(w.shape)
    shape[axis:axis + 1] = [RET_HEADS, RET_DV]
    w = w.reshape(shape)
    pad = [(0, 0)] * w.ndim
    pad[axis + 1] = (0, RET_DV_PAD - RET_DV)
    w = jnp.pad(w, pad)
    shape[axis:axis + 2] = [RET_HEADS * RET_DV_PAD]
    return w.reshape(shape)


def _odd_layer_mix(x2d, b, seq, norm_g, w_in, w_out, ret_g, lam_re, lam_im, log_dt, b_re, b_im,
                   c_re, c_im, d_skip, glu_w, glu_b, rope, log_gamma):
    hk = RET_HEADS * RET_DK
    hv = RET_HEADS * RET_DV
    w_perm = jnp.concatenate([w_in[:, :2 * hk],
                              _pad_heads(w_in[:, 2 * hk:2 * hk + hv], 1),
                              _pad_heads(w_in[:, 2 * hk + hv:2 * hk + 2 * hv], 1),
                              w_in[:, 2 * hk + 2 * hv:]], axis=1).astype(BF16)
    p2d = _norm_matmul(x2d, norm_g, w_perm)
    p = p2d.reshape(b, seq, ODD_COLS_PADDED)
    g_pad = _pad_heads(ret_g.astype(F32), 0).reshape(RET_HEADS, 1, RET_DV_PAD)
    yc = _retention_mixer(p, rope[0], rope[1], log_gamma, g_pad)

    n_chunks = seq // S5_CHUNK
    su = p[:, :, ODD_BLK["su"] * LANE:]
    u4 = su.reshape(b, n_chunks, S5_CHUNK, S5_GROUPS, S5_GROUP_CH).transpose(0, 3, 1, 2, 4)
    u4 = u4.reshape(b, S5_GROUPS, n_chunks, S5_CHUNK * S5_GROUP_CH)
    m, bst, cst, lam = _s5_operators(lam_re, lam_im, log_dt, b_re, b_im, c_re, c_im)
    y4 = _s5_core(u4, m, bst, cst, lam)
    y2d = y4.reshape(b, S5_GROUPS, n_chunks, S5_CHUNK, S5_GROUP_CH).transpose(0, 2, 3, 1, 4)
    y2d = y2d.reshape(b * seq, S5_WIDTH)
    yd = _s5_post(y2d, p2d, d_skip.astype(F32), glu_w, glu_b.astype(F32))

    t = b * seq
    w_ret = _pad_heads(w_out[:hv], 0).astype(BF16)
    return _out_proj(x2d, yc.reshape(t, -1), yd, w_ret, w_out[hv:].astype(BF16))


def _rope_tables(seq):
    half = RET_DK // 2
    inv = ROPE_BASE ** (-jnp.arange(half, dtype=F32) / half)
    ang = jnp.arange(seq, dtype=F32)[:, None] * inv[None, :]
    cos, sin = jnp.cos(ang), jnp.sin(ang)
    return jnp.concatenate([cos, cos], -1), jnp.concatenate([-sin, sin], -1)


def _retention_log_gamma(chunk):
    hidx = jnp.arange(RET_HEADS, dtype=F32)
    lg = jnp.stack([jnp.log1p(-jnp.exp2(-5.0 - hidx)), jnp.log1p(-jnp.exp2(-5.5 - hidx))], axis=1)
    return jnp.broadcast_to(lg[:, :, None], (RET_HEADS, 2, chunk))


def _hgrn_lower_bounds(lb_logits):
    p = jax.nn.softmax(lb_logits.astype(F32), axis=1)
    return jnp.cumsum(p, axis=1) - p[:, :1]


def kernel(x, mix_norm_g, ffn_norm_g, final_norm_g, w_in_even, w_out_even, gla_wa2, gla_ba, gla_norm_g,
           hgrn_lb_logits, hgrn_norm_g, w_in_odd, w_out_odd, ret_norm_g, s5_lam_re, s5_lam_im, s5_log_dt,
           s5_b_re, s5_b_im, s5_c_re, s5_c_im, s5_d, s5_glu_w, s5_glu_b,
           ffn_w_up, ffn_conv_w, ffn_conv_b, ffn_w_down):
    b, seq, d = x.shape
    lbs = _hgrn_lower_bounds(hgrn_lb_logits)
    rope = _rope_tables(seq)
    log_gamma = _retention_log_gamma(min(RET_CHUNK, seq))
    x2d = x.reshape(b * seq, d)
    for layer in range(DEPTH):
        j = layer // 2
        if layer % 2 == 0:
            x2d = _even_layer_mix(x2d, b, seq, mix_norm_g[layer], w_in_even[j], w_out_even[j], gla_wa2[j],
                                  gla_ba[j], gla_norm_g[j], lbs[0, j], lbs[1, j], hgrn_norm_g[j])
        else:
            x2d = _odd_layer_mix(x2d, b, seq, mix_norm_g[layer], w_in_odd[j], w_out_odd[j], ret_norm_g[j],
                                 s5_lam_re[j], s5_lam_im[j], s5_log_dt[j], s5_b_re[j], s5_b_im[j],
                                 s5_c_re[j], s5_c_im[j], s5_d[j], s5_glu_w[j], s5_glu_b[j], rope, log_gamma)
        x2d = _conv_ffn(x2d, seq, ffn_norm_g[layer], ffn_w_up[layer], ffn_conv_w[layer], ffn_conv_b[layer],
                        ffn_w_down[layer], final_norm_g, final_norm=(layer == DEPTH - 1))
    return x2d.reshape(b, seq, d)
```

```python
import functools
import math

import jax
import jax.numpy as jnp
import numpy as np
from jax import lax
from jax.experimental import pallas as pl
from jax.experimental.pallas import tpu as pltpu

F32 = jnp.float32
BF16 = jnp.bfloat16
HIGHEST = lax.Precision.HIGHEST

D_MODEL = 1024
DEPTH = 4
EPS = 1e-6

GLA_HEADS = 4
GLA_DK = 64
GLA_DV = 128
GLA_RANK = 16
GLA_GATE_NORM = 16.0
HGRN_HEADS = 4
HGRN_DK = 64
HGRN_DV = 128
HGRN_MIN_F = 1e-20

RET_HEADS = 4
RET_DK = 128
RET_DV = 192
RET_DV_PAD = 256
ROPE_BASE = 10000.0

S5_WIDTH = 256
S5_GROUP_CH = 16
S5_GROUPS = 16
S5_STATE = 64
S5_CHUNK = 16

FFN_DIM = 2816
FFN_CHUNK = 256
FFN_GROUP = 4 * FFN_CHUNK
CONV_WIDTH = 3

LANE = 128
SUBLANE = 8
HALO = 2 * SUBLANE
GATED_CHUNK = 64
GATED_BLOCK = 256
RET_CHUNK = 256
ROW_TILE = 512
FFN_ROW_TILE = 512
VMEM_LIMIT = 60 * 1024 * 1024

NT_DIMS = (((1,), (1,)), ((), ()))
TN_DIMS = (((0,), (0,)), ((), ()))


def _cparams(n_axes):
    return pltpu.CompilerParams(dimension_semantics=("arbitrary",) * n_axes,
                                vmem_limit_bytes=VMEM_LIMIT)


def _rms(x, g):
    return x * lax.rsqrt(jnp.mean(x * x, axis=-1, keepdims=True) + EPS) * g


def _sigmoid(x):
    return 1.0 / (1.0 + jnp.exp(-x))


def _silu(x):
    return x * _sigmoid(x)


def _log_sigmoid(z):
    return jnp.minimum(z, 0.0) - jnp.log(1.0 + jnp.exp(-jnp.abs(z)))


def _gelu_tanh(x):
    c = math.sqrt(2.0 / math.pi)
    return 0.5 * x * (1.0 + jnp.tanh(c * (x + 0.044715 * (x * x * x))))


def _norm_matmul_kernel(x_ref, g_ref, w_ref, o_ref, *, col_chunk):
    h = _rms(x_ref[...], g_ref[...]).astype(BF16)
    n_out = o_ref.shape[1]
    for j in range(n_out // col_chunk):
        cols = slice(j * col_chunk, (j + 1) * col_chunk)
        o_ref[:, cols] = jnp.dot(h, w_ref[:, cols], preferred_element_type=F32)


def _norm_matmul(x2d, g, w):
    t, d = x2d.shape
    e = w.shape[1]
    col_chunk = next(c * LANE for c in (4, 3, 2, 1) if e % (c * LANE) == 0)
    return pl.pallas_call(
        functools.partial(_norm_matmul_kernel, col_chunk=col_chunk),
        grid=(t // ROW_TILE,),
        in_specs=[pl.BlockSpec((ROW_TILE, d), lambda i: (i, 0)),
                  pl.BlockSpec((1, d), lambda i: (0, 0)),
                  pl.BlockSpec((d, e), lambda i: (0, 0))],
        out_specs=pl.BlockSpec((ROW_TILE, e), lambda i: (i, 0)),
        out_shape=jax.ShapeDtypeStruct((t, e), F32),
        compiler_params=_cparams(1),
        name="norm_in_proj",
    )(x2d, g.reshape(1, d), w)


def _decay_sum_matrix(block, chunk, forward):
    i = lax.broadcasted_iota(jnp.int32, (block, block), 0)
    t = lax.broadcasted_iota(jnp.int32, (block, block), 1)
    same = (i // chunk) == (t // chunk)
    return (same & ((t <= i) if forward else (t >= i))).astype(BF16)


def _chunk_cumsum(sum_mat, x):
    hi = x.astype(BF16)
    r1 = x - hi.astype(F32)
    mid = r1.astype(BF16)
    lo = (r1 - mid.astype(F32)).astype(BF16)
    s = jnp.dot(sum_mat, jnp.concatenate([hi, mid, lo], axis=1), preferred_element_type=F32)
    n = x.shape[1]
    return s[:, 0:n] + s[:, n:2 * n] + s[:, 2 * n:3 * n]


def _chunk_row(x, chunk, r, rows_out):
    picks = [jnp.broadcast_to(x[c * chunk + r:c * chunk + r + 1, :], (rows_out, x.shape[1]))
             for c in range(x.shape[0] // chunk)]
    return jnp.concatenate(picks, axis=0)


def _gated_prepare(q, k, la, sum_mat, bufs, rows, dec_rows, *, chunk, forward):
    qt_ref, kt_ref, qh_ref, kh_ref, dec_ref = bufs
    cum = _chunk_cumsum(sum_mat, la)
    ref = _chunk_row(cum, chunk, chunk // 2, chunk)
    tot_row = chunk - 1 if forward else 0
    tot = _chunk_row(cum, chunk, tot_row, chunk)
    qt_ref[rows, :] = (q * jnp.exp(cum - ref)).astype(BF16)
    kt_ref[rows, :] = (k * jnp.exp(ref - cum)).astype(BF16)
    qh_ref[rows, :] = (q * jnp.exp(cum)).astype(BF16)
    kh_ref[rows, :] = (k * jnp.exp(tot - cum)).astype(BF16)
    dec_ref[dec_rows, :] = jnp.exp(_chunk_row(cum, chunk, tot_row, SUBLANE))


def _gated_scan(bufs_f, bufs_b, vb_ref, o_f, o_b, st_f, st_b, *, seq, chunk):
    n_chunks = seq // chunk
    row = lax.broadcasted_iota(jnp.int32, (2 * chunk, chunk), 0) % chunk
    col = lax.broadcasted_iota(jnp.int32, (2 * chunk, chunk), 1)
    lower = row >= col
    upper = row <= col
    head0_lane = lax.broadcasted_iota(jnp.int32, (chunk, LANE), 1) < GLA_DK
    head0_val = lax.broadcasted_iota(jnp.int32, (chunk, 2 * LANE), 1) < LANE
    st_row = lax.broadcasted_iota(jnp.int32, (2 * LANE, LANE), 0) < LANE
    st_col = lax.broadcasted_iota(jnp.int32, (2 * LANE, LANE), 1) < GLA_DK
    st_mask = (st_row == st_col).astype(F32)

    st_f[...] = jnp.zeros_like(st_f)
    st_b[...] = jnp.zeros_like(st_b)

    def scores(c, forward):
        qt_ref, kt_ref = (bufs_f if forward else bufs_b)[0:2]
        rows = pl.ds(pl.multiple_of(c * chunk, chunk), chunk)
        qt = qt_ref[rows, :]
        zero = jnp.zeros_like(qt)
        q2 = jnp.concatenate([jnp.where(head0_lane, qt, zero), jnp.where(head0_lane, zero, qt)], axis=0)
        a = lax.dot_general(q2, kt_ref[rows, :], NT_DIMS, preferred_element_type=F32)
        return jnp.where(lower if forward else upper, a, 0.0).astype(BF16)

    def visit(c, forward, a):
        _, _, qh_ref, kh_ref, dec_ref = bufs_f if forward else bufs_b
        st_ref = st_f if forward else st_b
        o_ref = o_f if forward else o_b
        rows = pl.ds(pl.multiple_of(c * chunk, chunk), chunk)
        vb = vb_ref[rows, :]
        o2 = jnp.dot(a, vb, preferred_element_type=F32)
        o = jnp.where(head0_val, o2[:chunk, :], o2[chunk:, :])
        st = st_ref[...]
        o = o + lax.dot_general(qh_ref[rows, :], st.astype(BF16), NT_DIMS, preferred_element_type=F32)
        ds = lax.dot_general(vb, kh_ref[rows, :], TN_DIMS, preferred_element_type=F32)
        dec = dec_ref[pl.ds(pl.multiple_of(c * SUBLANE, SUBLANE), SUBLANE), :][0:1, :]
        st_ref[...] = st * dec + ds * st_mask
        o_ref[rows, :] = o

    def body(c, carry):
        nxt = jnp.minimum(c + 1, n_chunks - 1)
        ahead = (scores(nxt, True), scores(n_chunks - 1 - nxt, False))
        visit(c, True, carry[0])
        visit(n_chunks - 1 - c, False, carry[1])
        return ahead

    lax.fori_loop(0, n_chunks, body, (scores(0, True), scores(n_chunks - 1, False)), unroll=2)


def _gated_mixer_body(block_inputs, v_ref, gate_ref, g_ref, y_ref, scratch, *, seq, chunk):
    bufs_f, bufs_b, (vb_ref, o_f, o_b, st_f, st_b) = scratch[0:5], scratch[5:10], scratch[10:]
    block = min(GATED_BLOCK, seq)
    n_blocks = seq // block
    dec_per_block = block // chunk * SUBLANE
    sum_f = _decay_sum_matrix(block, chunk, True)
    sum_b = _decay_sum_matrix(block, chunk, False)

    def prepare(i, carry):
        rows = pl.ds(pl.multiple_of(i * block, block), block)
        dec_rows = pl.ds(pl.multiple_of(i * dec_per_block, dec_per_block), dec_per_block)
        q, k_f, k_b, la_f, la_b = block_inputs(rows)
        _gated_prepare(q, k_f, la_f, sum_f, bufs_f, rows, dec_rows, chunk=chunk, forward=True)
        _gated_prepare(q, k_b, la_b, sum_b, bufs_b, rows, dec_rows, chunk=chunk, forward=False)
        vb_ref[rows, :] = v_ref[rows, :].astype(BF16)
        return carry

    lax.fori_loop(0, n_blocks, prepare, 0)
    _gated_scan(bufs_f, bufs_b, vb_ref, o_f, o_b, st_f, st_b, seq=seq, chunk=chunk)

    g = g_ref[...]

    def finish(i, carry):
        rows = pl.ds(pl.multiple_of(i * block, block), block)
        o = o_f[rows, :] + o_b[rows, :]
        gate = gate_ref[rows, :]
        outs = []
        for h in range(2):
            cols = slice(h * LANE, (h + 1) * LANE)
            outs.append(_rms(o[:, cols], g[:, cols]) * _silu(gate[:, cols]))
        y_ref[rows, :] = jnp.concatenate(outs, axis=1).astype(y_ref.dtype)
        return carry

    lax.fori_loop(0, n_blocks, finish, 0)


def _gla_pair_kernel(q_ref, k_ref, v_ref, r_ref, lr_ref, wgf_ref, wgb_ref, baf_ref, bab_ref, g_ref,
                     y_ref, *scratch, seq, chunk):
    def block_inputs(rows):
        lr = lr_ref[rows, :].astype(BF16)
        las = []
        for w_ref, b_ref in ((wgf_ref, baf_ref), (wgb_ref, bab_ref)):
            z = jnp.dot(lr, w_ref[...], preferred_element_type=F32) + b_ref[...]
            las.append(_log_sigmoid(z) * (1.0 / GLA_GATE_NORM))
        k = k_ref[rows, :] * (GLA_DK ** -0.5)
        return q_ref[rows, :], k, k, las[0], las[1]

    _gated_mixer_body(block_inputs, v_ref, r_ref, g_ref, y_ref, scratch, seq=seq, chunk=chunk)


def _hgrn_pair_kernel(q_ref, zf_ref, zb_ref, v_ref, gate_ref, lbf_ref, lbb_ref, g_ref,
                      y_ref, *scratch, seq, chunk):
    def block_inputs(rows):
        ks, las = [], []
        for z_ref, lb_ref in ((zf_ref, lbf_ref), (zb_ref, lbb_ref)):
            z = z_ref[rows, :]
            lb = lb_ref[...]
            f = lb + (1.0 - lb) * _sigmoid(z)
            las.append(jnp.log(jnp.maximum(f, HGRN_MIN_F)))
            ks.append((1.0 - lb) * _sigmoid(-z))
        return _silu(q_ref[rows, :]), ks[0], ks[1], las[0], las[1]

    _gated_mixer_body(block_inputs, v_ref, gate_ref, g_ref, y_ref, scratch, seq=seq, chunk=chunk)


def _col_spec(seq, width, block_fn):
    return pl.BlockSpec((None, seq, width), lambda b, p: (b, 0, block_fn(p)))


def _pair_spec(shape):
    return pl.BlockSpec((None,) + shape, lambda b, p: (p,) + (0,) * len(shape))


def _gated_scratch(seq, chunk):
    per_dir = [pltpu.VMEM((seq, LANE), BF16) for _ in range(4)]
    per_dir.append(pltpu.VMEM((seq // chunk * SUBLANE, LANE), F32))
    return per_dir * 2 + [pltpu.VMEM((seq, 2 * LANE), BF16),
                          pltpu.VMEM((seq, 2 * LANE), F32), pltpu.VMEM((seq, 2 * LANE), F32),
                          pltpu.VMEM((2 * LANE, LANE), F32), pltpu.VMEM((2 * LANE, LANE), F32)]


EVEN_BLK = dict(gq=0, gk=2, gv=4, gr=8, hq=12, hzf=14, hzb=16, hi=18, hg=22, lr=26)
EVEN_COLS_PADDED = 27 * LANE


def _gla_mixer(p3, wgf, wgb, baf, bab, norm_g):
    b, seq, _ = p3.shape
    blk = EVEN_BLK
    kern = functools.partial(_gla_pair_kernel, seq=seq, chunk=GATED_CHUNK)
    return pl.pallas_call(
        kern, grid=(b, 2),
        in_specs=[_col_spec(seq, LANE, lambda p: blk["gq"] + p),
                  _col_spec(seq, LANE, lambda p: blk["gk"] + p),
                  _col_spec(seq, 2 * LANE, lambda p: blk["gv"] // 2 + p),
                  _col_spec(seq, 2 * LANE, lambda p: blk["gr"] // 2 + p),
                  _col_spec(seq, LANE, lambda p: blk["lr"]),
                  _pair_spec((LANE, LANE)), _pair_spec((LANE, LANE)),
                  _pair_spec((1, LANE)), _pair_spec((1, LANE)), _pair_spec((1, 2 * LANE))],
        out_specs=pl.BlockSpec((None, seq, 2 * LANE), lambda b, p: (b, 0, p)),
        out_shape=jax.ShapeDtypeStruct((b, seq, GLA_HEADS * GLA_DV), BF16),
        scratch_shapes=_gated_scratch(seq, GATED_CHUNK),
        compiler_params=_cparams(2),
        name="gla_mixer",
    )(p3, p3, p3, p3, p3, wgf, wgb, baf, bab, norm_g)


def _hgrn_mixer(p3, lbf, lbb, norm_g):
    b, seq, _ = p3.shape
    blk = EVEN_BLK
    kern = functools.partial(_hgrn_pair_kernel, seq=seq, chunk=GATED_CHUNK)
    return pl.pallas_call(
        kern, grid=(b, 2),
        in_specs=[_col_spec(seq, LANE, lambda p: blk["hq"] + p),
                  _col_spec(seq, LANE, lambda p: blk["hzf"] + p),
                  _col_spec(seq, LANE, lambda p: blk["hzb"] + p),
                  _col_spec(seq, 2 * LANE, lambda p: blk["hi"] // 2 + p),
                  _col_spec(seq, 2 * LANE, lambda p: blk["hg"] // 2 + p),
                  _pair_spec((1, LANE)), _pair_spec((1, LANE)), _pair_spec((1, 2 * LANE))],
        out_specs=pl.BlockSpec((None, seq, 2 * LANE), lambda b, p: (b, 0, p)),
        out_shape=jax.ShapeDtypeStruct((b, seq, HGRN_HEADS * HGRN_DV), BF16),
        scratch_shapes=_gated_scratch(seq, GATED_CHUNK),
        compiler_params=_cparams(2),
        name="hgrn2_mixer",
    )(p3, p3, p3, p3, p3, lbf, lbb, norm_g)


def _retention_kernel(q_ref, k_ref, v_ref, gate_ref, cos_ref, sin_ref, lg_ref, g_ref,
                      y_ref, qr, kr, o_acc, s_f, s_b, *, seq, chunk):
    half = RET_DK // 2
    cos = cos_ref[...]
    sin = sin_ref[...]
    q = q_ref[...]
    k = k_ref[...]
    qr[...] = q * cos + pltpu.roll(q, half, 1) * sin
    kr[...] = (k * cos + pltpu.roll(k, half, 1) * sin) * (RET_DK ** -0.5)

    lg_f = lg_ref[0:1, :]
    lg_b = lg_ref[1:2, :]
    n_chunks = seq // chunk
    row = lax.broadcasted_iota(jnp.int32, (chunk, chunk), 0)
    col = lax.broadcasted_iota(jnp.int32, (chunk, chunk), 1)
    diff = (row - col).astype(F32)
    dmat = (jnp.where(row >= col, jnp.exp(lg_f * diff), 0.0)
            + jnp.where(row <= col, jnp.exp(-lg_b * diff), 0.0))
    pos = lax.broadcasted_iota(jnp.int32, (chunk, LANE), 0).astype(F32)
    lgf = lg_f[:, 0:LANE]
    lgb = lg_b[:, 0:LANE]
    qdec_f = jnp.exp(lgf * (pos + 1.0))
    kdec_f = jnp.exp(lgf * (chunk - 1.0 - pos))
    qdec_b = jnp.exp(lgb * (chunk - pos))
    kdec_b = jnp.exp(lgb * pos)
    tot_f = jnp.exp(lgf[:, 0:1] * float(chunk))
    tot_b = jnp.exp(lgb[:, 0:1] * float(chunk))

    s_f[...] = jnp.zeros_like(s_f)
    s_b[...] = jnp.zeros_like(s_b)

    def body(c, carry):
        rows = pl.ds(pl.multiple_of(c * chunk, chunk), chunk)
        qc = qr[rows, :]
        kc = kr[rows, :]
        vc = v_ref[rows, :].astype(BF16)
        a = lax.dot_general(qc.astype(BF16), kc.astype(BF16), NT_DIMS, preferred_element_type=F32)
        o = jnp.dot((a * dmat).astype(BF16), vc, preferred_element_type=F32)
        s_old = s_f[...]
        o = o + jnp.dot((qc * qdec_f).astype(BF16), s_old.astype(BF16), preferred_element_type=F32)
        s_f[...] = s_old * tot_f + lax.dot_general((kc * kdec_f).astype(BF16), vc, TN_DIMS,
                                                   preferred_element_type=F32)
        o_acc[rows, :] = o
        return carry

    def body_b(c, carry):
        rows = pl.ds(pl.multiple_of((n_chunks - 1 - c) * chunk, chunk), chunk)
        qc = qr[rows, :]
        kc = kr[rows, :]
        vc = v_ref[rows, :].astype(BF16)
        s_old = s_b[...]
        o = jnp.dot((qc * qdec_b).astype(BF16), s_old.astype(BF16), preferred_element_type=F32)
        s_b[...] = s_old * tot_b + lax.dot_general((kc * kdec_b).astype(BF16), vc, TN_DIMS,
                                                   preferred_element_type=F32)
        o_acc[rows, :] = o_acc[rows, :] + o
        return carry

    lax.fori_loop(0, n_chunks, body, 0)
    lax.fori_loop(0, n_chunks, body_b, 0)

    o = o_acc[...]
    vlane = lax.broadcasted_iota(jnp.int32, (1, RET_DV_PAD), 1) < RET_DV
    mu = jnp.sum(o, axis=-1, keepdims=True) * (1.0 / RET_DV)
    cen = jnp.where(vlane, o - mu, 0.0)
    var = jnp.sum(cen * cen, axis=-1, keepdims=True) * (1.0 / RET_DV)
    y = cen * lax.rsqrt(var + EPS) * g_ref[...]
    y_ref[...] = (y * _silu(gate_ref[...])).astype(y_ref.dtype)


ODD_BLK = dict(rq=0, rk=4, rv=8, rg=16, su=24)
ODD_COLS_PADDED = 26 * LANE


def _retention_mixer(p3, cos2, sin2, log_gamma, norm_g):
    b, seq, _ = p3.shape
    blk = ODD_BLK
    chunk = min(RET_CHUNK, seq)
    kern = functools.partial(_retention_kernel, seq=seq, chunk=chunk)
    head = lambda shape: pl.BlockSpec((None,) + shape, lambda b, h: (h,) + (0,) * len(shape))
    table = pl.BlockSpec((seq, LANE), lambda b, h: (0, 0))
    return pl.pallas_call(
        kern, grid=(b, RET_HEADS),
        in_specs=[pl.BlockSpec((None, seq, LANE), lambda b, h: (b, 0, blk["rq"] + h)),
                  pl.BlockSpec((None, seq, LANE), lambda b, h: (b, 0, blk["rk"] + h)),
                  pl.BlockSpec((None, seq, RET_DV_PAD), lambda b, h: (b, 0, blk["rv"] // 2 + h)),
                  pl.BlockSpec((None, seq, RET_DV_PAD), lambda b, h: (b, 0, blk["rg"] // 2 + h)),
                  table, table, head((2, chunk)), head((1, RET_DV_PAD))],
        out_specs=pl.BlockSpec((None, seq, RET_DV_PAD), lambda b, h: (b, 0, h)),
        out_shape=jax.ShapeDtypeStruct((b, seq, RET_HEADS * RET_DV_PAD), BF16),
        scratch_shapes=[pltpu.VMEM((seq, LANE), F32), pltpu.VMEM((seq, LANE), F32),
                        pltpu.VMEM((seq, RET_DV_PAD), F32),
                        pltpu.VMEM((RET_DK, RET_DV_PAD), F32), pltpu.VMEM((RET_DK, RET_DV_PAD), F32)],
        compiler_params=_cparams(2),
        name="retention_mixer",
    )(p3, p3, p3, p3, cos2, sin2, log_gamma, norm_g)


def _s5_kernel(u_ref, m_ref, bst_ref, cst_ref, lam_ref, y_ref, *, n_chunks):
    u = u_ref[...].astype(BF16)
    x = jnp.dot(u, bst_ref[...], preferred_element_type=F32)
    half = S5_STATE
    row = lax.broadcasted_iota(jnp.int32, (n_chunks, LANE), 0)
    lane = lax.broadcasted_iota(jnp.int32, (1, LANE), 1)
    sign = jnp.where(lane < half, -1.0, 1.0)

    def cmul(z, a_re, a_im_signed):
        return z * a_re + pltpu.roll(z, half, 1) * a_im_signed

    hs = []
    for d in range(2):
        z = x[:, d * LANE:(d + 1) * LANE]
        a_re = lam_ref[2 * d:2 * d + 1, :]
        a_im = lam_ref[2 * d + 1:2 * d + 2, :] * sign
        step = 1
        while step < n_chunks:
            if d == 0:
                shifted = jnp.where(row >= step, pltpu.roll(z, step, 0), 0.0)
            else:
                shifted = jnp.where(row < n_chunks - step, pltpu.roll(z, n_chunks - step, 0), 0.0)
            z = z + cmul(shifted, a_re, a_im)
            a_re, a_im = a_re * a_re - a_im * a_im, 2.0 * a_re * a_im
            step *= 2
        if d == 0:
            z = jnp.where(row >= 1, pltpu.roll(z, 1, 0), 0.0)
        else:
            z = jnp.where(row < n_chunks - 1, pltpu.roll(z, n_chunks - 1, 0), 0.0)
        hs.append(z)
    h = jnp.concatenate(hs, axis=1).astype(BF16)
    y = jnp.dot(u, m_ref[...], preferred_element_type=F32)
    y = y + jnp.dot(h, cst_ref[...], preferred_element_type=F32)
    y_ref[...] = y


def _s5_core(u4, m, bst, cst, lam):
    b, g, n_chunks, w = u4.shape
    per_group = lambda shape: pl.BlockSpec((None,) + shape, lambda gi, bi: (gi,) + (0,) * len(shape))
    return pl.pallas_call(
        functools.partial(_s5_kernel, n_chunks=n_chunks),
        grid=(g, b),
        in_specs=[pl.BlockSpec((None, None, n_chunks, w), lambda gi, bi: (bi, gi, 0, 0)),
                  per_group((w, w)), per_group((w, w)), per_group((w, w)), per_group((4, LANE))],
        out_specs=pl.BlockSpec((None, None, n_chunks, w), lambda gi, bi: (bi, gi, 0, 0)),
        out_shape=jax.ShapeDtypeStruct((b, g, n_chunks, w), F32),
        compiler_params=_cparams(2),
        name="s5_core",
    )(u4, m, bst, cst, lam)


def _s5_post_kernel(y_ref, u_ref, d_ref, w_ref, b_ref, o_ref):
    y = y_ref[...] + d_ref[...] * u_ref[...]
    g = _gelu_tanh(y)
    z = jnp.dot(g.astype(BF16), w_ref[...], preferred_element_type=F32) + b_ref[...]
    o_ref[...] = (g * _sigmoid(z)).astype(o_ref.dtype)


def _s5_post(y2d, p2d, d_skip, glu_w, glu_b):
    t, w = y2d.shape
    row = lambda i: (i, 0)
    const = lambda i: (0, 0)
    return pl.pallas_call(
        _s5_post_kernel, grid=(t // ROW_TILE,),
        in_specs=[pl.BlockSpec((ROW_TILE, w), row),
                  pl.BlockSpec((ROW_TILE, w), lambda i: (i, ODD_BLK["su"] // 2)),
                  pl.BlockSpec((1, w), const), pl.BlockSpec((w, w), const), pl.BlockSpec((1, w), const)],
        out_specs=pl.BlockSpec((ROW_TILE, w), row),
        out_shape=jax.ShapeDtypeStruct((t, w), BF16),
        compiler_params=_cparams(1),
        name="s5_glu",
    )(y2d, p2d, d_skip.reshape(1, w), glu_w.astype(BF16), glu_b.reshape(1, w))


def _s5_operators(lam_re, lam_im, log_dt, b_re, b_im, c_re, c_im):
    L = S5_CHUNK
    steps = jnp.arange(L, dtype=F32)
    cre, cim = c_re.astype(F32), c_im.astype(F32)
    c_stack = jnp.concatenate([cre, cim], axis=-1)
    cre_t, cim_t = cre.transpose(0, 2, 1), cim.transpose(0, 2, 1)

    t_sum = 0.0
    bst_parts, cst_parts, lam_rows = [], [], []
    for d in range(2):
        lr = jnp.minimum(lam_re[d].astype(F32), -1e-4)
        li = lam_im[d].astype(F32)
        dt = jnp.exp(log_dt[d].astype(F32))[:, None]
        a, w = lr * dt, li * dt
        mag = jnp.exp(a)
        ar, ai = mag * jnp.cos(w), mag * jnp.sin(w)
        den = lr * lr + li * li
        nr = ar - 1.0
        cr = (nr * lr + ai * li) / den
        ci = (ai * lr - nr * li) / den
        br, bi = b_re.astype(F32).transpose(0, 2, 1), b_im.astype(F32).transpose(0, 2, 1)
        bbr = cr[:, None, :] * br - ci[:, None, :] * bi
        bbi = cr[:, None, :] * bi + ci[:, None, :] * br

        def power(lag, a=a, w=w):
            m_ = jnp.exp(a * lag)
            return m_ * jnp.cos(w * lag), m_ * jnp.sin(w * lag)

        jj = steps[None, :, None, None, None]
        ii = steps[None, None, None, :, None]
        lag = jnp.maximum(ii - jj, 0.0) if d == 0 else jnp.maximum(jj - ii, 0.0)
        valid = (ii >= jj) if d == 0 else (jj >= ii)
        a5, w5 = a[:, None, None, None, :], w[:, None, None, None, :]
        pr, pi = power(lag, a5, w5)
        b5r, b5i = bbr[:, None, :, None, :], bbi[:, None, :, None, :]
        sr = jnp.where(valid, pr * b5r - pi * b5i, 0.0)
        si = jnp.where(valid, pr * b5i + pi * b5r, 0.0)
        t_sum = t_sum + jnp.concatenate([sr, -si], axis=-1)

        lag_s = (L - 1.0 - steps) if d == 0 else steps
        pr, pi = power(lag_s[None, :, None, None], a[:, None, None, :], w[:, None, None, :])
        bst_parts += [pr * bbr[:, None] - pi * bbi[:, None], pr * bbi[:, None] + pi * bbr[:, None]]

        lag_c = (steps + 1.0) if d == 0 else (L - steps)
        pr, pi = power(lag_c[None, None, :, None], a[:, :, None, None], w[:, :, None, None])
        cst_parts += [cre_t[:, :, None, :] * pr - cim_t[:, :, None, :] * pi,
                      -cre_t[:, :, None, :] * pi - cim_t[:, :, None, :] * pr]

        pr_l, pi_l = power(float(L))
        lam_rows += [jnp.concatenate([pr_l, pr_l], -1), jnp.concatenate([pi_l, pi_l], -1)]

    m = jnp.einsum('gjpin,gqn->gjpiq', t_sum, c_stack, precision=HIGHEST)
    m = m.reshape(S5_GROUPS, L * S5_GROUP_CH, L * S5_GROUP_CH)
    bst = jnp.stack(bst_parts, axis=3).reshape(S5_GROUPS, L * S5_GROUP_CH, 4 * S5_STATE)
    cst = jnp.stack(cst_parts, axis=1).reshape(S5_GROUPS, 4 * S5_STATE, L * S5_GROUP_CH)
    lam = jnp.stack(lam_rows, 1)
    return m.astype(BF16), bst.astype(BF16), cst.astype(BF16), lam


def _proj_ffn_kernel(x_ref, xp_ref, xn_ref, ya_ref, yap_ref, yan_ref, yb_ref, ybp_ref, ybn_ref,
                     wa_ref, wb_ref, g_ref, wup_ref, cw_ref, cb_ref, wdn_ref, gfin_ref,
                     o_ref, act_ref, *, tiles_per_seq, final_norm):
    tm = x_ref.shape[0]
    ext = tm + 2 * HALO
    i = pl.program_id(0)
    first = (i % tiles_per_seq) == 0
    last = (i % tiles_per_seq) == tiles_per_seq - 1
    rows3 = lambda p, m, n: jnp.concatenate([p[...], m[...], n[...]], axis=0)
    x1 = rows3(xp_ref, x_ref, xn_ref)
    x1 = x1 + jnp.dot(rows3(yap_ref, ya_ref, yan_ref), wa_ref[...], preferred_element_type=F32)
    x1 = x1 + jnp.dot(rows3(ybp_ref, yb_ref, ybn_ref), wb_ref[...], preferred_element_type=F32)
    r = lax.broadcasted_iota(jnp.int32, (ext, 1), 0)
    outside = ((r < HALO) & first) | ((r >= tm + HALO) & last)
    h = jnp.where(outside, 0.0, _rms(x1, g_ref[...])).astype(BF16)
    down = []
    for j0 in range(0, FFN_DIM, FFN_GROUP):
        width = min(FFN_GROUP, FFN_DIM - j0)
        for j in range(j0, j0 + width, FFN_CHUNK):
            gated = []
            for base in (0, FFN_DIM):
                cols = slice(base + j, base + j + FFN_CHUNK)
                u = jnp.dot(h, wup_ref[:, cols], preferred_element_type=F32)
                u_prev = pltpu.roll(u, 1, 0)[HALO:HALO + tm, :]
                u_next = pltpu.roll(u, ext - 1, 0)[HALO:HALO + tm, :]
                c = cb_ref[:, cols] + u_prev * cw_ref[0:1, cols]
                c = c + u[HALO:HALO + tm, :] * cw_ref[1:2, cols]
                c = c + u_next * cw_ref[2:3, cols]
                gated.append(c)
            act_ref[:, j - j0:j - j0 + FFN_CHUNK] = (_silu(gated[0]) * gated[1]).astype(BF16)
        down.append(jnp.dot(act_ref[:, 0:width], wdn_ref[j0:j0 + width, :], preferred_element_type=F32))
    out = x1[HALO:HALO + tm, :] + functools.reduce(lambda a, b: a + b, down)
    if final_norm:
        out = _rms(out, gfin_ref[...])
    o_ref[...] = out


def _proj_ffn(x2d, seq, ya, yb, wa, wb, g, w_up, conv_w, conv_b, w_down, g_final, final_norm):
    t, d = x2d.shape
    tm = min(FFN_ROW_TILE, seq)
    tiles_per_seq = seq // tm
    halos_per_tile = tm // HALO
    n_halo_blocks = t // HALO

    def tile3(width):
        return [pl.BlockSpec((tm, width), lambda i: (i, 0)),
                pl.BlockSpec((HALO, width), lambda i: (jnp.maximum(i * halos_per_tile - 1, 0), 0)),
                pl.BlockSpec((HALO, width),
                             lambda i: (jnp.minimum((i + 1) * halos_per_tile, n_halo_blocks - 1), 0))]

    def resident(a):
        return pl.BlockSpec(a.shape, lambda i: (0,) * a.ndim)

    kern = functools.partial(_proj_ffn_kernel, tiles_per_seq=tiles_per_seq, final_norm=final_norm)
    consts = [wa, wb, g.reshape(1, d), w_up.astype(BF16), conv_w, conv_b.reshape(1, 2 * FFN_DIM),
              w_down.astype(BF16), g_final.reshape(1, d)]
    return pl.pallas_call(
        kern, grid=(t // tm,),
        in_specs=tile3(d) + tile3(ya.shape[1]) + tile3(yb.shape[1]) + [resident(c) for c in consts],
        out_specs=pl.BlockSpec((tm, d), lambda i: (i, 0)),
        out_shape=jax.ShapeDtypeStruct((t, d), F32),
        scratch_shapes=[pltpu.VMEM((tm, FFN_GROUP), BF16)],
        compiler_params=_cparams(1),
        name="proj_conv_ffn",
    )(x2d, x2d, x2d, ya, ya, ya, yb, yb, yb, *consts)


def _pad_cols(w, width):
    return jnp.pad(w, ((0, 0), (0, width - w.shape[1])))


def _even_layer_mix(x2d, b, seq, norm_g, w_in, w_out, wa2, ba, gla_g, lb_f, lb_b, hgrn_g):
    sizes = (256, 256, 512, 512, 16, 16, 256, 256, 256, 512, 512)
    offs = np.concatenate([[0], np.cumsum(sizes)])
    gq, gk, gv, gr, glf, glb, hq, hzf, hzb, hi, hg = (slice(offs[i], offs[i + 1]) for i in range(11))
    w_perm = jnp.concatenate([w_in[:, s] for s in (gq, gk, gv, gr, hq, hzf, hzb, hi, hg, glf, glb)], axis=1)
    w_perm = _pad_cols(w_perm, EVEN_COLS_PADDED).astype(BF16)
    p = _norm_matmul(x2d, norm_g, w_perm).reshape(b, seq, EVEN_COLS_PADDED)

    def gate_w(direction):
        w = jnp.zeros((LANE, GLA_HEADS * GLA_DK), F32)
        w = w.at[direction * GLA_RANK:(direction + 1) * GLA_RANK, :].set(wa2[direction].astype(F32))
        return w.reshape(LANE, 2, LANE).transpose(1, 0, 2).astype(BF16)

    baf = ba[0].astype(F32).reshape(2, 1, LANE)
    bab = ba[1].astype(F32).reshape(2, 1, LANE)
    ya = _gla_mixer(p, gate_w(0), gate_w(1), baf, bab, gla_g.astype(F32).reshape(2, 1, 2 * LANE))
    yb = _hgrn_mixer(p, lb_f.reshape(2, 1, LANE), lb_b.reshape(2, 1, LANE),
                     hgrn_g.astype(F32).reshape(2, 1, 2 * LANE))
    t = b * seq
    n_a = GLA_HEADS * GLA_DV
    return ya.reshape(t, n_a), yb.reshape(t, -1), w_out[:n_a].astype(BF16), w_out[n_a:].astype(BF16)


def _pad_heads(w, axis):
    shape = list(w.shape)
    shape[axis:axis + 1] = [RET_HEADS, RET_DV]
    w = w.reshape(shape)
    pad = [(0, 0)] * w.ndim
    pad[axis + 1] = (0, RET_DV_PAD - RET_DV)
    w = jnp.pad(w, pad)
    shape[axis:axis + 2] = [RET_HEADS * RET_DV_PAD]
    return w.reshape(shape)


def _odd_layer_mix(x2d, b, seq, norm_g, w_in, w_out, ret_g, lam_re, lam_im, log_dt, b_re, b_im,
                   c_re, c_im, d_skip, glu_w, glu_b, rope, log_gamma):
    hk = RET_HEADS * RET_DK
    hv = RET_HEADS * RET_DV
    w_perm = jnp.concatenate([w_in[:, :2 * hk],
                              _pad_heads(w_in[:, 2 * hk:2 * hk + hv], 1),
                              _pad_heads(w_in[:, 2 * hk + hv:2 * hk + 2 * hv], 1),
                              w_in[:, 2 * hk + 2 * hv:]], axis=1).astype(BF16)
    p2d = _norm_matmul(x2d, norm_g, w_perm)
    p = p2d.reshape(b, seq, ODD_COLS_PADDED)
    g_pad = _pad_heads(ret_g.astype(F32), 0).reshape(RET_HEADS, 1, RET_DV_PAD)
    yc = _retention_mixer(p, rope[0], rope[1], log_gamma, g_pad)

    n_chunks = seq // S5_CHUNK
    su = p[:, :, ODD_BLK["su"] * LANE:]
    u4 = su.reshape(b, n_chunks, S5_CHUNK, S5_GROUPS, S5_GROUP_CH).transpose(0, 3, 1, 2, 4)
    u4 = u4.reshape(b, S5_GROUPS, n_chunks, S5_CHUNK * S5_GROUP_CH)
    m, bst, cst, lam = _s5_operators(lam_re, lam_im, log_dt, b_re, b_im, c_re, c_im)
    y4 = _s5_core(u4, m, bst, cst, lam)
    y2d = y4.reshape(b, S5_GROUPS, n_chunks, S5_CHUNK, S5_GROUP_CH).transpose(0, 2, 3, 1, 4)
    y2d = y2d.reshape(b * seq, S5_WIDTH)
    yd = _s5_post(y2d, p2d, d_skip.astype(F32), glu_w, glu_b.astype(F32))

    t = b * seq
    w_ret = _pad_heads(w_out[:hv], 0).astype(BF16)
    return yc.reshape(t, -1), yd, w_ret, w_out[hv:].astype(BF16)


def _rope_tables(seq):
    half = RET_DK // 2
    inv = ROPE_BASE ** (-jnp.arange(half, dtype=F32) / half)
    ang = jnp.arange(seq, dtype=F32)[:, None] * inv[None, :]
    cos, sin = jnp.cos(ang), jnp.sin(ang)
    return jnp.concatenate([cos, cos], -1), jnp.concatenate([-sin, sin], -1)


def _retention_log_gamma(chunk):
    hidx = jnp.arange(RET_HEADS, dtype=F32)
    lg = jnp.stack([jnp.log1p(-jnp.exp2(-5.0 - hidx)), jnp.log1p(-jnp.exp2(-5.5 - hidx))], axis=1)
    return jnp.broadcast_to(lg[:, :, None], (RET_HEADS, 2, chunk))


def _hgrn_lower_bounds(lb_logits):
    p = jax.nn.softmax(lb_logits.astype(F32), axis=1)
    return jnp.cumsum(p, axis=1) - p[:, :1]


def kernel(x, mix_norm_g, ffn_norm_g, final_norm_g, w_in_even, w_out_even, gla_wa2, gla_ba, gla_norm_g,
           hgrn_lb_logits, hgrn_norm_g, w_in_odd, w_out_odd, ret_norm_g, s5_lam_re, s5_lam_im, s5_log_dt,
           s5_b_re, s5_b_im, s5_c_re, s5_c_im, s5_d, s5_glu_w, s5_glu_b,
           ffn_w_up, ffn_conv_w, ffn_conv_b, ffn_w_down):
    b, seq, d = x.shape
    lbs = _hgrn_lower_bounds(hgrn_lb_logits)
    rope = _rope_tables(seq)
    log_gamma = _retention_log_gamma(min(RET_CHUNK, seq))
    x2d = x.reshape(b * seq, d)
    for layer in range(DEPTH):
        j = layer // 2
        if layer % 2 == 0:
            mix = _even_layer_mix(x2d, b, seq, mix_norm_g[layer], w_in_even[j], w_out_even[j], gla_wa2[j],
                                  gla_ba[j], gla_norm_g[j], lbs[0, j], lbs[1, j], hgrn_norm_g[j])
        else:
            mix = _odd_layer_mix(x2d, b, seq, mix_norm_g[layer], w_in_odd[j], w_out_odd[j], ret_norm_g[j],
                                 s5_lam_re[j], s5_lam_im[j], s5_log_dt[j], s5_b_re[j], s5_b_im[j],
                                 s5_c_re[j], s5_c_im[j], s5_d[j], s5_glu_w[j], s5_glu_b[j], rope, log_gamma)
        x2d = _proj_ffn(x2d, seq, *mix, ffn_norm_g[layer], ffn_w_up[layer], ffn_conv_w[layer],
                        ffn_conv_b[layer], ffn_w_down[layer], final_norm_g, final_norm=(layer == DEPTH - 1))
    return x2d.reshape(b, seq, d)
```

```python
import functools
import math

import jax
import jax.numpy as jnp
import numpy as np
from jax import lax
from jax.experimental import pallas as pl
from jax.experimental.pallas import tpu as pltpu

F32 = jnp.float32
BF16 = jnp.bfloat16
HIGHEST = lax.Precision.HIGHEST

D_MODEL = 1024
DEPTH = 4
EPS = 1e-6

GLA_HEADS = 4
GLA_DK = 64
GLA_DV = 128
GLA_RANK = 16
GLA_GATE_NORM = 16.0
HGRN_HEADS = 4
HGRN_DK = 64
HGRN_DV = 128
HGRN_MIN_F = 1e-20

RET_HEADS = 4
RET_DK = 128
RET_DV = 192
RET_DV_PAD = 256
ROPE_BASE = 10000.0

S5_WIDTH = 256
S5_GROUP_CH = 16
S5_GROUPS = 16
S5_STATE = 64
S5_CHUNK = 8

FFN_DIM = 2816
FFN_CHUNK = 256
FFN_GROUP = 4 * FFN_CHUNK
CONV_WIDTH = 3

LANE = 128
SUBLANE = 8
HALO = 2 * SUBLANE
GATED_CHUNK = 64
GATED_BLOCK = 256
RET_CHUNK = 256
ROW_TILE = 512
FFN_ROW_TILE = 1024
VMEM_LIMIT = 60 * 1024 * 1024

NT_DIMS = (((1,), (1,)), ((), ()))
TN_DIMS = (((0,), (0,)), ((), ()))


def _cparams(n_axes):
    return pltpu.CompilerParams(dimension_semantics=("arbitrary",) * n_axes,
                                vmem_limit_bytes=VMEM_LIMIT)


def _rms(x, g):
    return x * lax.rsqrt(jnp.mean(x * x, axis=-1, keepdims=True) + EPS) * g


def _sigmoid(x):
    return 1.0 / (1.0 + jnp.exp(-x))


def _silu(x):
    return x * _sigmoid(x)


def _log_sigmoid(z):
    return jnp.minimum(z, 0.0) - jnp.log(1.0 + jnp.exp(-jnp.abs(z)))


def _gelu_tanh(x):
    c = math.sqrt(2.0 / math.pi)
    return 0.5 * x * (1.0 + jnp.tanh(c * (x + 0.044715 * (x * x * x))))


def _norm_matmul_kernel(x_ref, g_ref, w_ref, o_ref, *, col_chunk):
    h = _rms(x_ref[...], g_ref[...]).astype(BF16)
    n_out = o_ref.shape[1]
    for j in range(n_out // col_chunk):
        cols = slice(j * col_chunk, (j + 1) * col_chunk)
        o_ref[:, cols] = jnp.dot(h, w_ref[:, cols], preferred_element_type=F32)


def _norm_matmul(x2d, g, w):
    t, d = x2d.shape
    e = w.shape[1]
    col_chunk = next(c * LANE for c in (4, 3, 2, 1) if e % (c * LANE) == 0)
    return pl.pallas_call(
        functools.partial(_norm_matmul_kernel, col_chunk=col_chunk),
        grid=(t // ROW_TILE,),
        in_specs=[pl.BlockSpec((ROW_TILE, d), lambda i: (i, 0)),
                  pl.BlockSpec((1, d), lambda i: (0, 0)),
                  pl.BlockSpec((d, e), lambda i: (0, 0))],
        out_specs=pl.BlockSpec((ROW_TILE, e), lambda i: (i, 0)),
        out_shape=jax.ShapeDtypeStruct((t, e), F32),
        compiler_params=_cparams(1),
        name="norm_in_proj",
    )(x2d, g.reshape(1, d), w)


def _decay_sum_matrix(block, chunk, forward):
    i = lax.broadcasted_iota(jnp.int32, (block, block), 0)
    t = lax.broadcasted_iota(jnp.int32, (block, block), 1)
    same = (i // chunk) == (t // chunk)
    return (same & ((t <= i) if forward else (t >= i))).astype(BF16)


def _chunk_cumsum(sum_mat, x):
    hi = x.astype(BF16)
    r1 = x - hi.astype(F32)
    mid = r1.astype(BF16)
    lo = (r1 - mid.astype(F32)).astype(BF16)
    s = jnp.dot(sum_mat, jnp.concatenate([hi, mid, lo], axis=1), preferred_element_type=F32)
    n = x.shape[1]
    return s[:, 0:n] + s[:, n:2 * n] + s[:, 2 * n:3 * n]


def _chunk_row(x, chunk, r, rows_out):
    picks = [jnp.broadcast_to(x[c * chunk + r:c * chunk + r + 1, :], (rows_out, x.shape[1]))
             for c in range(x.shape[0] // chunk)]
    return jnp.concatenate(picks, axis=0)


def _gated_prepare(q, k, la, sum_mat, bufs, rows, dec_rows, *, chunk, forward):
    qt_ref, kt_ref, qh_ref, kh_ref, dec_ref = bufs
    cum = _chunk_cumsum(sum_mat, la)
    ref = _chunk_row(cum, chunk, chunk // 2, chunk)
    tot_row = chunk - 1 if forward else 0
    tot = _chunk_row(cum, chunk, tot_row, chunk)
    qt_ref[rows, :] = (q * jnp.exp(cum - ref)).astype(BF16)
    kt_ref[rows, :] = (k * jnp.exp(ref - cum)).astype(BF16)
    qh_ref[rows, :] = (q * jnp.exp(cum)).astype(BF16)
    kh_ref[rows, :] = (k * jnp.exp(tot - cum)).astype(BF16)
    dec_ref[dec_rows, :] = jnp.exp(_chunk_row(cum, chunk, tot_row, SUBLANE))


def _gated_scan(bufs_f, bufs_b, vb_ref, o_f, o_b, st_f, st_b, *, seq, chunk):
    n_chunks = seq // chunk
    row = lax.broadcasted_iota(jnp.int32, (2 * chunk, chunk), 0) % chunk
    col = lax.broadcasted_iota(jnp.int32, (2 * chunk, chunk), 1)
    lower = row >= col
    upper = row <= col
    head0_lane = lax.broadcasted_iota(jnp.int32, (chunk, LANE), 1) < GLA_DK
    head0_val = lax.broadcasted_iota(jnp.int32, (chunk, 2 * LANE), 1) < LANE
    st_row = lax.broadcasted_iota(jnp.int32, (2 * LANE, LANE), 0) < LANE
    st_col = lax.broadcasted_iota(jnp.int32, (2 * LANE, LANE), 1) < GLA_DK
    st_mask = (st_row == st_col).astype(F32)

    st_f[...] = jnp.zeros_like(st_f)
    st_b[...] = jnp.zeros_like(st_b)

    def scores(c, forward):
        qt_ref, kt_ref = (bufs_f if forward else bufs_b)[0:2]
        rows = pl.ds(pl.multiple_of(c * chunk, chunk), chunk)
        qt = qt_ref[rows, :]
        zero = jnp.zeros_like(qt)
        q2 = jnp.concatenate([jnp.where(head0_lane, qt, zero), jnp.where(head0_lane, zero, qt)], axis=0)
        a = lax.dot_general(q2, kt_ref[rows, :], NT_DIMS, preferred_element_type=F32)
        return jnp.where(lower if forward else upper, a, 0.0).astype(BF16)

    def visit(c, forward, a):
        _, _, qh_ref, kh_ref, dec_ref = bufs_f if forward else bufs_b
        st_ref = st_f if forward else st_b
        o_ref = o_f if forward else o_b
        rows = pl.ds(pl.multiple_of(c * chunk, chunk), chunk)
        vb = vb_ref[rows, :]
        o2 = jnp.dot(a, vb, preferred_element_type=F32)
        o = jnp.where(head0_val, o2[:chunk, :], o2[chunk:, :])
        st = st_ref[...]
        o = o + lax.dot_general(qh_ref[rows, :], st.astype(BF16), NT_DIMS, preferred_element_type=F32)
        ds = lax.dot_general(vb, kh_ref[rows, :], TN_DIMS, preferred_element_type=F32)
        dec = dec_ref[pl.ds(pl.multiple_of(c * SUBLANE, SUBLANE), SUBLANE), :][0:1, :]
        st_ref[...] = st * dec + ds * st_mask
        o_ref[rows, :] = o

    def body(c, carry):
        nxt = jnp.minimum(c + 1, n_chunks - 1)
        ahead = (scores(nxt, True), scores(n_chunks - 1 - nxt, False))
        visit(c, True, carry[0])
        visit(n_chunks - 1 - c, False, carry[1])
        return ahead

    lax.fori_loop(0, n_chunks, body, (scores(0, True), scores(n_chunks - 1, False)), unroll=2)


def _gated_mixer_body(block_inputs, v_ref, gate_ref, g_ref, y_ref, scratch, *, seq, chunk):
    bufs_f, bufs_b, (vb_ref, o_f, o_b, st_f, st_b) = scratch[0:5], scratch[5:10], scratch[10:]
    block = min(GATED_BLOCK, seq)
    n_blocks = seq // block
    dec_per_block = block // chunk * SUBLANE
    sum_f = _decay_sum_matrix(block, chunk, True)
    sum_b = _decay_sum_matrix(block, chunk, False)

    def prepare(i, carry):
        rows = pl.ds(pl.multiple_of(i * block, block), block)
        dec_rows = pl.ds(pl.multiple_of(i * dec_per_block, dec_per_block), dec_per_block)
        q, k_f, k_b, la_f, la_b = block_inputs(rows)
        _gated_prepare(q, k_f, la_f, sum_f, bufs_f, rows, dec_rows, chunk=chunk, forward=True)
        _gated_prepare(q, k_b, la_b, sum_b, bufs_b, rows, dec_rows, chunk=chunk, forward=False)
        vb_ref[rows, :] = v_ref[rows, :].astype(BF16)
        return carry

    lax.fori_loop(0, n_blocks, prepare, 0)
    _gated_scan(bufs_f, bufs_b, vb_ref, o_f, o_b, st_f, st_b, seq=seq, chunk=chunk)

    g = g_ref[...]

    def finish(i, carry):
        rows = pl.ds(pl.multiple_of(i * block, block), block)
        o = o_f[rows, :] + o_b[rows, :]
        gate = gate_ref[rows, :]
        outs = []
        for h in range(2):
            cols = slice(h * LANE, (h + 1) * LANE)
            outs.append(_rms(o[:, cols], g[:, cols]) * _silu(gate[:, cols]))
        y_ref[rows, :] = jnp.concatenate(outs, axis=1).astype(y_ref.dtype)
        return carry

    lax.fori_loop(0, n_blocks, finish, 0)


def _gla_pair_kernel(q_ref, k_ref, v_ref, r_ref, lr_ref, wgf_ref, wgb_ref, baf_ref, bab_ref, g_ref,
                     y_ref, *scratch, seq, chunk):
    def block_inputs(rows):
        lr = lr_ref[rows, :].astype(BF16)
        las = []
        for w_ref, b_ref in ((wgf_ref, baf_ref), (wgb_ref, bab_ref)):
            z = jnp.dot(lr, w_ref[...], preferred_element_type=F32) + b_ref[...]
            las.append(_log_sigmoid(z) * (1.0 / GLA_GATE_NORM))
        k = k_ref[rows, :] * (GLA_DK ** -0.5)
        return q_ref[rows, :], k, k, las[0], las[1]

    _gated_mixer_body(block_inputs, v_ref, r_ref, g_ref, y_ref, scratch, seq=seq, chunk=chunk)


def _hgrn_pair_kernel(q_ref, zf_ref, zb_ref, v_ref, gate_ref, lbf_ref, lbb_ref, g_ref,
                      y_ref, *scratch, seq, chunk):
    def block_inputs(rows):
        ks, las = [], []
        for z_ref, lb_ref in ((zf_ref, lbf_ref), (zb_ref, lbb_ref)):
            z = z_ref[rows, :]
            lb = lb_ref[...]
            f = lb + (1.0 - lb) * _sigmoid(z)
            las.append(jnp.log(jnp.maximum(f, HGRN_MIN_F)))
            ks.append((1.0 - lb) * _sigmoid(-z))
        return _silu(q_ref[rows, :]), ks[0], ks[1], las[0], las[1]

    _gated_mixer_body(block_inputs, v_ref, gate_ref, g_ref, y_ref, scratch, seq=seq, chunk=chunk)


def _col_spec(seq, width, block_fn):
    return pl.BlockSpec((None, seq, width), lambda b, p: (b, 0, block_fn(p)))


def _pair_spec(shape):
    return pl.BlockSpec((None,) + shape, lambda b, p: (p,) + (0,) * len(shape))


def _gated_scratch(seq, chunk):
    per_dir = [pltpu.VMEM((seq, LANE), BF16) for _ in range(4)]
    per_dir.append(pltpu.VMEM((seq // chunk * SUBLANE, LANE), F32))
    return per_dir * 2 + [pltpu.VMEM((seq, 2 * LANE), BF16),
                          pltpu.VMEM((seq, 2 * LANE), F32), pltpu.VMEM((seq, 2 * LANE), F32),
                          pltpu.VMEM((2 * LANE, LANE), F32), pltpu.VMEM((2 * LANE, LANE), F32)]


EVEN_BLK = dict(gq=0, gk=2, gv=4, gr=8, hq=12, hzf=14, hzb=16, hi=18, hg=22, lr=26)
EVEN_COLS_PADDED = 27 * LANE


def _gla_mixer(p3, wgf, wgb, baf, bab, norm_g):
    b, seq, _ = p3.shape
    blk = EVEN_BLK
    kern = functools.partial(_gla_pair_kernel, seq=seq, chunk=GATED_CHUNK)
    return pl.pallas_call(
        kern, grid=(b, 2),
        in_specs=[_col_spec(seq, LANE, lambda p: blk["gq"] + p),
                  _col_spec(seq, LANE, lambda p: blk["gk"] + p),
                  _col_spec(seq, 2 * LANE, lambda p: blk["gv"] // 2 + p),
                  _col_spec(seq, 2 * LANE, lambda p: blk["gr"] // 2 + p),
                  _col_spec(seq, LANE, lambda p: blk["lr"]),
                  _pair_spec((LANE, LANE)), _pair_spec((LANE, LANE)),
                  _pair_spec((1, LANE)), _pair_spec((1, LANE)), _pair_spec((1, 2 * LANE))],
        out_specs=pl.BlockSpec((None, seq, 2 * LANE), lambda b, p: (b, 0, p)),
        out_shape=jax.ShapeDtypeStruct((b, seq, GLA_HEADS * GLA_DV), BF16),
        scratch_shapes=_gated_scratch(seq, GATED_CHUNK),
        compiler_params=_cparams(2),
        name="gla_mixer",
    )(p3, p3, p3, p3, p3, wgf, wgb, baf, bab, norm_g)


def _hgrn_mixer(p3, lbf, lbb, norm_g):
    b, seq, _ = p3.shape
    blk = EVEN_BLK
    kern = functools.partial(_hgrn_pair_kernel, seq=seq, chunk=GATED_CHUNK)
    return pl.pallas_call(
        kern, grid=(b, 2),
        in_specs=[_col_spec(seq, LANE, lambda p: blk["hq"] + p),
                  _col_spec(seq, LANE, lambda p: blk["hzf"] + p),
                  _col_spec(seq, LANE, lambda p: blk["hzb"] + p),
                  _col_spec(seq, 2 * LANE, lambda p: blk["hi"] // 2 + p),
                  _col_spec(seq, 2 * LANE, lambda p: blk["hg"] // 2 + p),
                  _pair_spec((1, LANE)), _pair_spec((1, LANE)), _pair_spec((1, 2 * LANE))],
        out_specs=pl.BlockSpec((None, seq, 2 * LANE), lambda b, p: (b, 0, p)),
        out_shape=jax.ShapeDtypeStruct((b, seq, HGRN_HEADS * HGRN_DV), BF16),
        scratch_shapes=_gated_scratch(seq, GATED_CHUNK),
        compiler_params=_cparams(2),
        name="hgrn2_mixer",
    )(p3, p3, p3, p3, p3, lbf, lbb, norm_g)


def _retention_kernel(q_ref, k_ref, v_ref, gate_ref, cos_ref, sin_ref, lg_ref, g_ref,
                      y_ref, qr, kr, qf, kf, qb, kb, vb, a_buf, o_f, o_b, s_f, s_b, *, seq, chunk):
    half = RET_DK // 2
    lg_f = lg_ref[0:1, :]
    lg_b = lg_ref[1:2, :]
    n_chunks = seq // chunk
    row = lax.broadcasted_iota(jnp.int32, (chunk, chunk), 0)
    col = lax.broadcasted_iota(jnp.int32, (chunk, chunk), 1)
    diff = (row - col).astype(F32)
    dmat = (jnp.where(row >= col, jnp.exp(lg_f * diff), 0.0)
            + jnp.where(row <= col, jnp.exp(-lg_b * diff), 0.0))
    pos = lax.broadcasted_iota(jnp.int32, (chunk, LANE), 0).astype(F32)
    lgf = lg_f[:, 0:LANE]
    lgb = lg_b[:, 0:LANE]
    qdec_f = jnp.exp(lgf * (pos + 1.0))
    kdec_f = jnp.exp(lgf * (chunk - 1.0 - pos))
    qdec_b = jnp.exp(lgb * (chunk - pos))
    kdec_b = jnp.exp(lgb * pos)
    tot_f = jnp.exp(lgf[:, 0:1] * float(chunk))
    tot_b = jnp.exp(lgb[:, 0:1] * float(chunk))

    def chunk_rows(c):
        return pl.ds(pl.multiple_of(c * chunk, chunk), chunk)

    def prepare(c, carry):
        rows = chunk_rows(c)
        cos = cos_ref[rows, :]
        sin = sin_ref[rows, :]
        q = q_ref[rows, :]
        k = k_ref[rows, :]
        q_rot = q * cos + pltpu.roll(q, half, 1) * sin
        k_rot = (k * cos + pltpu.roll(k, half, 1) * sin) * (RET_DK ** -0.5)
        qr[rows, :] = q_rot.astype(BF16)
        kr[rows, :] = k_rot.astype(BF16)
        qf[rows, :] = (q_rot * qdec_f).astype(BF16)
        kf[rows, :] = (k_rot * kdec_f).astype(BF16)
        qb[rows, :] = (q_rot * qdec_b).astype(BF16)
        kb[rows, :] = (k_rot * kdec_b).astype(BF16)
        vb[rows, :] = v_ref[rows, :].astype(BF16)
        return carry

    lax.fori_loop(0, n_chunks, prepare, 0)

    def scores(c):
        rows = chunk_rows(c)
        a = lax.dot_general(qr[rows, :], kr[rows, :], NT_DIMS, preferred_element_type=F32)
        return (a * dmat).astype(BF16)

    s_f[...] = jnp.zeros_like(s_f)
    s_b[...] = jnp.zeros_like(s_b)
    a_buf[0] = scores(0)

    def visit(c, slot):
        a_buf[1 - slot] = scores(jnp.minimum(c + 1, n_chunks - 1))
        rows = chunk_rows(c)
        vc = vb[rows, :]
        s_old = s_f[...]
        o = jnp.dot(a_buf[slot], vc, preferred_element_type=F32)
        o_f[rows, :] = o + jnp.dot(qf[rows, :], s_old.astype(BF16), preferred_element_type=F32)
        s_f[...] = s_old * tot_f + lax.dot_general(kf[rows, :], vc, TN_DIMS, preferred_element_type=F32)

        rows_b = chunk_rows(n_chunks - 1 - c)
        vc_b = vb[rows_b, :]
        s_old_b = s_b[...]
        o_b[rows_b, :] = jnp.dot(qb[rows_b, :], s_old_b.astype(BF16), preferred_element_type=F32)
        s_b[...] = s_old_b * tot_b + lax.dot_general(kb[rows_b, :], vc_b, TN_DIMS,
                                                     preferred_element_type=F32)

    def body(i, carry):
        visit(2 * i, 0)
        visit(2 * i + 1, 1)
        return carry

    lax.fori_loop(0, n_chunks // 2, body, 0)

    g = g_ref[...]
    vlane = lax.broadcasted_iota(jnp.int32, (1, RET_DV_PAD), 1) < RET_DV

    def finish(c, carry):
        rows = chunk_rows(c)
        o = o_f[rows, :] + o_b[rows, :]
        mu = jnp.sum(o, axis=-1, keepdims=True) * (1.0 / RET_DV)
        cen = jnp.where(vlane, o - mu, 0.0)
        var = jnp.sum(cen * cen, axis=-1, keepdims=True) * (1.0 / RET_DV)
        y = cen * lax.rsqrt(var + EPS) * g
        y_ref[rows, :] = (y * _silu(gate_ref[rows, :])).astype(y_ref.dtype)
        return carry

    lax.fori_loop(0, n_chunks, finish, 0)


ODD_BLK = dict(rq=0, rk=4, rv=8, rg=16, su=24)
ODD_COLS_PADDED = 26 * LANE


def _retention_mixer(p3, cos2, sin2, log_gamma, norm_g):
    b, seq, _ = p3.shape
    blk = ODD_BLK
    chunk = min(RET_CHUNK, seq)
    kern = functools.partial(_retention_kernel, seq=seq, chunk=chunk)
    head = lambda shape: pl.BlockSpec((None,) + shape, lambda b, h: (h,) + (0,) * len(shape))
    table = pl.BlockSpec((seq, LANE), lambda b, h: (0, 0))
    return pl.pallas_call(
        kern, grid=(b, RET_HEADS),
        in_specs=[pl.BlockSpec((None, seq, LANE), lambda b, h: (b, 0, blk["rq"] + h)),
                  pl.BlockSpec((None, seq, LANE), lambda b, h: (b, 0, blk["rk"] + h)),
                  pl.BlockSpec((None, seq, RET_DV_PAD), lambda b, h: (b, 0, blk["rv"] // 2 + h)),
                  pl.BlockSpec((None, seq, RET_DV_PAD), lambda b, h: (b, 0, blk["rg"] // 2 + h)),
                  table, table, head((2, chunk)), head((1, RET_DV_PAD))],
        out_specs=pl.BlockSpec((None, seq, RET_DV_PAD), lambda b, h: (b, 0, h)),
        out_shape=jax.ShapeDtypeStruct((b, seq, RET_HEADS * RET_DV_PAD), BF16),
        scratch_shapes=([pltpu.VMEM((seq, LANE), BF16) for _ in range(6)]
                        + [pltpu.VMEM((seq, RET_DV_PAD), BF16),
                           pltpu.VMEM((2, chunk, chunk), BF16),
                           pltpu.VMEM((seq, RET_DV_PAD), F32), pltpu.VMEM((seq, RET_DV_PAD), F32),
                           pltpu.VMEM((RET_DK, RET_DV_PAD), F32), pltpu.VMEM((RET_DK, RET_DV_PAD), F32)]),
        compiler_params=_cparams(2),
        name="retention_mixer",
    )(p3, p3, p3, p3, cos2, sin2, log_gamma, norm_g)


def _s5_state_kernel(u_ref, b_ref, lam_ref, h_ref, xs_ref, hs_ref, *, n_rows):
    x = jnp.dot(u_ref[...].astype(BF16), b_ref[...], preferred_element_type=F32)
    n_tiles = 2 * S5_GROUPS
    for t in range(n_tiles):
        xs_ref[t] = x[:, t * LANE:(t + 1) * LANE]
    half = S5_STATE
    row = lax.broadcasted_iota(jnp.int32, (n_rows, LANE), 0)
    lane = lax.broadcasted_iota(jnp.int32, (1, LANE), 1)
    sign = jnp.where(lane < half, -1.0, 1.0)

    def cmul(z, a_re, a_im_signed):
        return z * a_re + pltpu.roll(z, half, 1) * a_im_signed

    def group(g, carry):
        lam = lam_ref[g]
        for d in range(2):
            z = xs_ref[d * S5_GROUPS + g]
            a_re = lam[2 * d:2 * d + 1, :]
            a_im = lam[2 * d + 1:2 * d + 2, :] * sign
            step = 1
            while step < n_rows:
                if d == 0:
                    shifted = jnp.where(row >= step, pltpu.roll(z, step, 0), 0.0)
                else:
                    shifted = jnp.where(row < n_rows - step, pltpu.roll(z, n_rows - step, 0), 0.0)
                z = z + cmul(shifted, a_re, a_im)
                a_re, a_im = a_re * a_re - a_im * a_im, 2.0 * a_re * a_im
                step *= 2
            if d == 0:
                z = jnp.where(row >= 1, pltpu.roll(z, 1, 0), 0.0)
            else:
                z = jnp.where(row < n_rows - 1, pltpu.roll(z, n_rows - 1, 0), 0.0)
            hs_ref[d * S5_GROUPS + g] = z.astype(BF16)
        return carry

    lax.fori_loop(0, S5_GROUPS, group, 0)
    for t in range(n_tiles):
        h_ref[:, t * LANE:(t + 1) * LANE] = hs_ref[t]


def _s5_out_kernel(u_ref, h_ref, m_ref, c_ref, y_ref):
    y = jnp.dot(u_ref[...].astype(BF16), m_ref[...], preferred_element_type=F32)
    y_ref[...] = y + jnp.dot(h_ref[...], c_ref[...], preferred_element_type=F32)


def _s5_core(u3, m, bst, cst, lam):
    b, n_rows, w = u3.shape
    n_state = bst.shape[1]
    batch = lambda width: pl.BlockSpec((None, n_rows, width), lambda bi: (bi, 0, 0))
    full = lambda a: pl.BlockSpec(a.shape, lambda bi: (0,) * a.ndim)
    h = pl.pallas_call(
        functools.partial(_s5_state_kernel, n_rows=n_rows),
        grid=(b,),
        in_specs=[batch(w), full(bst), full(lam)],
        out_specs=batch(n_state),
        out_shape=jax.ShapeDtypeStruct((b, n_rows, n_state), BF16),
        scratch_shapes=[pltpu.VMEM((n_state // LANE, n_rows, LANE), F32),
                        pltpu.VMEM((n_state // LANE, n_rows, LANE), BF16)],
        compiler_params=_cparams(1),
        name="s5_state",
    )(u3, bst, lam)
    return pl.pallas_call(
        _s5_out_kernel, grid=(b,),
        in_specs=[batch(w), batch(n_state), full(m), full(cst)],
        out_specs=batch(w),
        out_shape=jax.ShapeDtypeStruct((b, n_rows, w), F32),
        compiler_params=_cparams(1),
        name="s5_out",
    )(u3, h, m, cst)


def _s5_post_kernel(y_ref, u_ref, d_ref, w_ref, b_ref, o_ref):
    y = y_ref[...] + d_ref[...] * u_ref[...]
    g = _gelu_tanh(y)
    z = jnp.dot(g.astype(BF16), w_ref[...], preferred_element_type=F32) + b_ref[...]
    o_ref[...] = (g * _sigmoid(z)).astype(o_ref.dtype)


def _s5_post(y2d, p2d, d_skip, glu_w, glu_b):
    t, w = y2d.shape
    row = lambda i: (i, 0)
    const = lambda i: (0, 0)
    return pl.pallas_call(
        _s5_post_kernel, grid=(t // ROW_TILE,),
        in_specs=[pl.BlockSpec((ROW_TILE, w), row),
                  pl.BlockSpec((ROW_TILE, w), lambda i: (i, ODD_BLK["su"] // 2)),
                  pl.BlockSpec((1, w), const), pl.BlockSpec((w, w), const), pl.BlockSpec((1, w), const)],
        out_specs=pl.BlockSpec((ROW_TILE, w), row),
        out_shape=jax.ShapeDtypeStruct((t, w), BF16),
        compiler_params=_cparams(1),
        name="s5_glu",
    )(y2d, p2d, d_skip.reshape(1, w), glu_w.astype(BF16), glu_b.reshape(1, w))


def _s5_operators(lam_re, lam_im, log_dt, b_re, b_im, c_re, c_im):
    L = S5_CHUNK
    steps = jnp.arange(L, dtype=F32)
    cre, cim = c_re.astype(F32), c_im.astype(F32)
    c_stack = jnp.concatenate([cre, cim], axis=-1)
    cre_t, cim_t = cre.transpose(0, 2, 1), cim.transpose(0, 2, 1)

    t_sum = 0.0
    bst_parts, cst_parts, lam_rows = [], [], []
    for d in range(2):
        lr = jnp.minimum(lam_re[d].astype(F32), -1e-4)
        li = lam_im[d].astype(F32)
        dt = jnp.exp(log_dt[d].astype(F32))[:, None]
        a, w = lr * dt, li * dt
        mag = jnp.exp(a)
        ar, ai = mag * jnp.cos(w), mag * jnp.sin(w)
        den = lr * lr + li * li
        nr = ar - 1.0
        cr = (nr * lr + ai * li) / den
        ci = (ai * lr - nr * li) / den
        br, bi = b_re.astype(F32).transpose(0, 2, 1), b_im.astype(F32).transpose(0, 2, 1)
        bbr = cr[:, None, :] * br - ci[:, None, :] * bi
        bbi = cr[:, None, :] * bi + ci[:, None, :] * br

        def power(lag, a=a, w=w):
            m_ = jnp.exp(a * lag)
            return m_ * jnp.cos(w * lag), m_ * jnp.sin(w * lag)

        jj = steps[None, :, None, None, None]
        ii = steps[None, None, None, :, None]
        lag = jnp.maximum(ii - jj, 0.0) if d == 0 else jnp.maximum(jj - ii, 0.0)
        valid = (ii >= jj) if d == 0 else (jj >= ii)
        a5, w5 = a[:, None, None, None, :], w[:, None, None, None, :]
        pr, pi = power(lag, a5, w5)
        b5r, b5i = bbr[:, None, :, None, :], bbi[:, None, :, None, :]
        sr = jnp.where(valid, pr * b5r - pi * b5i, 0.0)
        si = jnp.where(valid, pr * b5i + pi * b5r, 0.0)
        t_sum = t_sum + jnp.concatenate([sr, -si], axis=-1)

        lag_s = (L - 1.0 - steps) if d == 0 else steps
        pr, pi = power(lag_s[None, :, None, None], a[:, None, None, :], w[:, None, None, :])
        bst_parts += [pr * bbr[:, None] - pi * bbi[:, None], pr * bbi[:, None] + pi * bbr[:, None]]

        lag_c = (steps + 1.0) if d == 0 else (L - steps)
        pr, pi = power(lag_c[None, None, :, None], a[:, :, None, None], w[:, :, None, None])
        cst_parts += [cre_t[:, :, None, :] * pr - cim_t[:, :, None, :] * pi,
                      -cre_t[:, :, None, :] * pi - cim_t[:, :, None, :] * pr]

        pr_l, pi_l = power(float(L))
        lam_rows += [jnp.concatenate([pr_l, pr_l], -1), jnp.concatenate([pi_l, pi_l], -1)]

    m = jnp.einsum('gjpin,gqn->gjpiq', t_sum, c_stack, precision=HIGHEST)
    bst = jnp.stack(bst_parts, axis=3).reshape(S5_GROUPS, L, S5_GROUP_CH, 2, 2, S5_STATE)
    cst = jnp.stack(cst_parts, axis=1).reshape(S5_GROUPS, 2, 2, S5_STATE, L, S5_GROUP_CH)
    lam = jnp.stack(lam_rows, 1)

    eye = jnp.eye(S5_GROUPS, dtype=F32)
    width = L * S5_WIDTH
    n_state = 4 * S5_GROUPS * S5_STATE
    m_full = jnp.einsum('gjpiq,gh->jgpihq', m, eye).reshape(width, width)
    b_full = jnp.einsum('gjpdrn,gh->jgpdhrn', bst, eye).reshape(width, n_state)
    c_full = jnp.einsum('gdrniq,gh->dgrnihq', cst, eye).reshape(n_state, width)
    return m_full.astype(BF16), b_full.astype(BF16), c_full.astype(BF16), lam


def _proj_ffn_kernel(x_ref, xp_ref, xn_ref, ya_ref, yap_ref, yan_ref, yb_ref, ybp_ref, ybn_ref,
                     wa_ref, wb_ref, g_ref, wup_ref, cw_ref, cb_ref, wdn_ref, gfin_ref,
                     o_ref, act_ref, *, tiles_per_seq, final_norm):
    tm = x_ref.shape[0]
    ext = tm + 2 * HALO
    i = pl.program_id(0)
    first = (i % tiles_per_seq) == 0
    last = (i % tiles_per_seq) == tiles_per_seq - 1
    rows3 = lambda p, m, n: jnp.concatenate([p[...], m[...], n[...]], axis=0)
    x1 = rows3(xp_ref, x_ref, xn_ref)
    x1 = x1 + jnp.dot(rows3(yap_ref, ya_ref, yan_ref), wa_ref[...], preferred_element_type=F32)
    x1 = x1 + jnp.dot(rows3(ybp_ref, yb_ref, ybn_ref), wb_ref[...], preferred_element_type=F32)
    r = lax.broadcasted_iota(jnp.int32, (ext, 1), 0)
    outside = ((r < HALO) & first) | ((r >= tm + HALO) & last)
    h = jnp.where(outside, 0.0, _rms(x1, g_ref[...])).astype(BF16)
    down = []
    for j0 in range(0, FFN_DIM, FFN_GROUP):
        width = min(FFN_GROUP, FFN_DIM - j0)
        for j in range(j0, j0 + width, FFN_CHUNK):
            gated = []
            for base in (0, FFN_DIM):
                cols = slice(base + j, base + j + FFN_CHUNK)
                u = jnp.dot(h, wup_ref[:, cols], preferred_element_type=F32)
                u_prev = pltpu.roll(u, 1, 0)[HALO:HALO + tm, :]
                u_next = pltpu.roll(u, ext - 1, 0)[HALO:HALO + tm, :]
                c = cb_ref[:, cols] + u_prev * cw_ref[0:1, cols]
                c = c + u[HALO:HALO + tm, :] * cw_ref[1:2, cols]
                c = c + u_next * cw_ref[2:3, cols]
                gated.append(c)
            act_ref[:, j - j0:j - j0 + FFN_CHUNK] = (_silu(gated[0]) * gated[1]).astype(BF16)
        down.append(jnp.dot(act_ref[:, 0:width], wdn_ref[j0:j0 + width, :], preferred_element_type=F32))
    out = x1[HALO:HALO + tm, :] + functools.reduce(lambda a, b: a + b, down)
    if final_norm:
        out = _rms(out, gfin_ref[...])
    o_ref[...] = out


def _proj_ffn(x2d, seq, ya, yb, wa, wb, g, w_up, conv_w, conv_b, w_down, g_final, final_norm):
    t, d = x2d.shape
    tm = min(FFN_ROW_TILE, seq)
    tiles_per_seq = seq // tm
    halos_per_tile = tm // HALO
    n_halo_blocks = t // HALO

    def tile3(width):
        return [pl.BlockSpec((tm, width), lambda i: (i, 0)),
                pl.BlockSpec((HALO, width), lambda i: (jnp.maximum(i * halos_per_tile - 1, 0), 0)),
                pl.BlockSpec((HALO, width),
                             lambda i: (jnp.minimum((i + 1) * halos_per_tile, n_halo_blocks - 1), 0))]

    def resident(a):
        return pl.BlockSpec(a.shape, lambda i: (0,) * a.ndim)

    kern = functools.partial(_proj_ffn_kernel, tiles_per_seq=tiles_per_seq, final_norm=final_norm)
    consts = [wa, wb, g.reshape(1, d), w_up.astype(BF16), conv_w, conv_b.reshape(1, 2 * FFN_DIM),
              w_down.astype(BF16), g_final.reshape(1, d)]
    return pl.pallas_call(
        kern, grid=(t // tm,),
        in_specs=tile3(d) + tile3(ya.shape[1]) + tile3(yb.shape[1]) + [resident(c) for c in consts],
        out_specs=pl.BlockSpec((tm, d), lambda i: (i, 0)),
        out_shape=jax.ShapeDtypeStruct((t, d), F32),
        scratch_shapes=[pltpu.VMEM((tm, FFN_GROUP), BF16)],
        compiler_params=_cparams(1),
        name="proj_conv_ffn",
    )(x2d, x2d, x2d, ya, ya, ya, yb, yb, yb, *consts)


def _pad_cols(w, width):
    return jnp.pad(w, ((0, 0), (0, width - w.shape[1])))


def _even_layer_mix(x2d, b, seq, norm_g, w_in, w_out, wa2, ba, gla_g, lb_f, lb_b, hgrn_g):
    sizes = (256, 256, 512, 512, 16, 16, 256, 256, 256, 512, 512)
    offs = np.concatenate([[0], np.cumsum(sizes)])
    gq, gk, gv, gr, glf, glb, hq, hzf, hzb, hi, hg = (slice(offs[i], offs[i + 1]) for i in range(11))
    w_perm = jnp.concatenate([w_in[:, s] for s in (gq, gk, gv, gr, hq, hzf, hzb, hi, hg, glf, glb)], axis=1)
    w_perm = _pad_cols(w_perm, EVEN_COLS_PADDED).astype(BF16)
    p = _norm_matmul(x2d, norm_g, w_perm).reshape(b, seq, EVEN_COLS_PADDED)

    def gate_w(direction):
        w = jnp.zeros((LANE, GLA_HEADS * GLA_DK), F32)
        w = w.at[direction * GLA_RANK:(direction + 1) * GLA_RANK, :].set(wa2[direction].astype(F32))
        return w.reshape(LANE, 2, LANE).transpose(1, 0, 2).astype(BF16)

    baf = ba[0].astype(F32).reshape(2, 1, LANE)
    bab = ba[1].astype(F32).reshape(2, 1, LANE)
    ya = _gla_mixer(p, gate_w(0), gate_w(1), baf, bab, gla_g.astype(F32).reshape(2, 1, 2 * LANE))
    yb = _hgrn_mixer(p, lb_f.reshape(2, 1, LANE), lb_b.reshape(2, 1, LANE),
                     hgrn_g.astype(F32).reshape(2, 1, 2 * LANE))
    t = b * seq
    n_a = GLA_HEADS * GLA_DV
    return ya.reshape(t, n_a), yb.reshape(t, -1), w_out[:n_a].astype(BF16), w_out[n_a:].astype(BF16)


def _pad_heads(w, axis):
    shape = list(w.shape)
    shape[axis:axis + 1] = [RET_HEADS, RET_DV]
    w = w.reshape(shape)
    pad = [(0, 0)] * w.ndim
    pad[axis + 1] = (0, RET_DV_PAD - RET_DV)
    w = jnp.pad(w, pad)
    shape[axis:axis + 2] = [RET_HEADS * RET_DV_PAD]
    return w.reshape(shape)


def _odd_layer_mix(x2d, b, seq, norm_g, w_in, w_out, ret_g, lam_re, lam_im, log_dt, b_re, b_im,
                   c_re, c_im, d_skip, glu_w, glu_b, rope, log_gamma):
    hk = RET_HEADS * RET_DK
    hv = RET_HEADS * RET_DV
    w_perm = jnp.concatenate([w_in[:, :2 * hk],
                              _pad_heads(w_in[:, 2 * hk:2 * hk + hv], 1),
                              _pad_heads(w_in[:, 2 * hk + hv:2 * hk + 2 * hv], 1),
                              w_in[:, 2 * hk + 2 * hv:]], axis=1).astype(BF16)
    p2d = _norm_matmul(x2d, norm_g, w_perm)
    p = p2d.reshape(b, seq, ODD_COLS_PADDED)
    g_pad = _pad_heads(ret_g.astype(F32), 0).reshape(RET_HEADS, 1, RET_DV_PAD)
    yc = _retention_mixer(p, rope[0], rope[1], log_gamma, g_pad)

    su = p2d[:, ODD_BLK["su"] * LANE:]
    u3 = su.reshape(b, seq // S5_CHUNK, S5_CHUNK * S5_WIDTH)
    m, bst, cst, lam = _s5_operators(lam_re, lam_im, log_dt, b_re, b_im, c_re, c_im)
    y2d = _s5_core(u3, m, bst, cst, lam).reshape(b * seq, S5_WIDTH)
    yd = _s5_post(y2d, p2d, d_skip.astype(F32), glu_w, glu_b.astype(F32))

    t = b * seq
    w_ret = _pad_heads(w_out[:hv], 0).astype(BF16)
    return yc.reshape(t, -1), yd, w_ret, w_out[hv:].astype(BF16)


def _rope_tables(seq):
    half = RET_DK // 2
    inv = ROPE_BASE ** (-jnp.arange(half, dtype=F32) / half)
    ang = jnp.arange(seq, dtype=F32)[:, None] * inv[None, :]
    cos, sin = jnp.cos(ang), jnp.sin(ang)
    return jnp.concatenate([cos, cos], -1), jnp.concatenate([-sin, sin], -1)


def _retention_log_gamma(chunk):
    hidx = jnp.arange(RET_HEADS, dtype=F32)
    lg = jnp.stack([jnp.log1p(-jnp.exp2(-5.0 - hidx)), jnp.log1p(-jnp.exp2(-5.5 - hidx))], axis=1)
    return jnp.broadcast_to(lg[:, :, None], (RET_HEADS, 2, chunk))


def _hgrn_lower_bounds(lb_logits):
    p = jax.nn.softmax(lb_logits.astype(F32), axis=1)
    return jnp.cumsum(p, axis=1) - p[:, :1]


def kernel(x, mix_norm_g, ffn_norm_g, final_norm_g, w_in_even, w_out_even, gla_wa2, gla_ba, gla_norm_g,
           hgrn_lb_logits, hgrn_norm_g, w_in_odd, w_out_odd, ret_norm_g, s5_lam_re, s5_lam_im, s5_log_dt,
           s5_b_re, s5_b_im, s5_c_re, s5_c_im, s5_d, s5_glu_w, s5_glu_b,
           ffn_w_up, ffn_conv_w, ffn_conv_b, ffn_w_down):
    b, seq, d = x.shape
    lbs = _hgrn_lower_bounds(hgrn_lb_logits)
    rope = _rope_tables(seq)
    log_gamma = _retention_log_gamma(min(RET_CHUNK, seq))
    x2d = x.reshape(b * seq, d)
    for layer in range(DEPTH):
        j = layer // 2
        if layer % 2 == 0:
            mix = _even_layer_mix(x2d, b, seq, mix_norm_g[layer], w_in_even[j], w_out_even[j], gla_wa2[j],
                                  gla_ba[j], gla_norm_g[j], lbs[0, j], lbs[1, j], hgrn_norm_g[j])
        else:
            mix = _odd_layer_mix(x2d, b, seq, mix_norm_g[layer], w_in_odd[j], w_out_odd[j], ret_norm_g[j],
                                 s5_lam_re[j], s5_lam_im[j], s5_log_dt[j], s5_b_re[j], s5_b_im[j],
                                 s5_c_re[j], s5_c_im[j], s5_d[j], s5_glu_w[j], s5_glu_b[j], rope, log_gamma)
        x2d = _proj_ffn(x2d, seq, *mix, ffn_norm_g[layer], ffn_w_up[layer], ffn_conv_w[layer],
                        ffn_conv_b[layer], ffn_w_down[layer], final_norm_g, final_norm=(layer == DEPTH - 1))
    return x2d.reshape(b, seq, d)
```

```python
import functools
import math

import jax
import jax.numpy as jnp
import numpy as np
from jax import lax
from jax.experimental import pallas as pl
from jax.experimental.pallas import tpu as pltpu

F32 = jnp.float32
BF16 = jnp.bfloat16
HIGHEST = lax.Precision.HIGHEST

D_MODEL = 1024
DEPTH = 4
EPS = 1e-6

GLA_HEADS = 4
GLA_DK = 64
GLA_DV = 128
GLA_RANK = 16
GLA_GATE_NORM = 16.0
HGRN_HEADS = 4
HGRN_DK = 64
HGRN_DV = 128
HGRN_MIN_F = 1e-20

RET_HEADS = 4
RET_DK = 128
RET_DV = 192
RET_DV_PAD = 256
ROPE_BASE = 10000.0

S5_WIDTH = 256
S5_GROUP_CH = 16
S5_GROUPS = 16
S5_STATE = 64
S5_CHUNK = 8

FFN_DIM = 2816
FFN_CHUNK = 256
FFN_GROUP = 4 * FFN_CHUNK
CONV_WIDTH = 3

LANE = 128
SUBLANE = 8
HALO = 2 * SUBLANE
GATED_CHUNK = 64
GATED_BLOCK = 256
RET_CHUNK = 256
ROW_TILE = 512
FFN_ROW_TILE = 1024
VMEM_LIMIT = 60 * 1024 * 1024

NT_DIMS = (((1,), (1,)), ((), ()))
TN_DIMS = (((0,), (0,)), ((), ()))


def _cparams(n_axes):
    return pltpu.CompilerParams(dimension_semantics=("arbitrary",) * n_axes,
                                vmem_limit_bytes=VMEM_LIMIT)


def _rms(x, g):
    return x * lax.rsqrt(jnp.mean(x * x, axis=-1, keepdims=True) + EPS) * g


def _sigmoid(x):
    return 1.0 / (1.0 + jnp.exp(-x))


def _silu(x):
    return x * _sigmoid(x)


def _log_sigmoid(z):
    return jnp.minimum(z, 0.0) - jnp.log(1.0 + jnp.exp(-jnp.abs(z)))


def _gelu_tanh(x):
    c = math.sqrt(2.0 / math.pi)
    return 0.5 * x * (1.0 + jnp.tanh(c * (x + 0.044715 * (x * x * x))))


def _norm_matmul_kernel(x_ref, g_ref, w_ref, o_ref, *, col_chunk):
    h = _rms(x_ref[...], g_ref[...]).astype(BF16)
    n_out = o_ref.shape[1]
    for j in range(n_out // col_chunk):
        cols = slice(j * col_chunk, (j + 1) * col_chunk)
        o_ref[:, cols] = jnp.dot(h, w_ref[:, cols], preferred_element_type=F32)


def _norm_matmul(x2d, g, w):
    t, d = x2d.shape
    e = w.shape[1]
    col_chunk = next(c * LANE for c in (4, 3, 2, 1) if e % (c * LANE) == 0)
    return pl.pallas_call(
        functools.partial(_norm_matmul_kernel, col_chunk=col_chunk),
        grid=(t // ROW_TILE,),
        in_specs=[pl.BlockSpec((ROW_TILE, d), lambda i: (i, 0)),
                  pl.BlockSpec((1, d), lambda i: (0, 0)),
                  pl.BlockSpec((d, e), lambda i: (0, 0))],
        out_specs=pl.BlockSpec((ROW_TILE, e), lambda i: (i, 0)),
        out_shape=jax.ShapeDtypeStruct((t, e), F32),
        compiler_params=_cparams(1),
        name="norm_in_proj",
    )(x2d, g.reshape(1, d), w)


def _decay_sum_matrix(block, chunk, forward):
    i = lax.broadcasted_iota(jnp.int32, (block, block), 0)
    t = lax.broadcasted_iota(jnp.int32, (block, block), 1)
    same = (i // chunk) == (t // chunk)
    return (same & ((t <= i) if forward else (t >= i))).astype(BF16)


def _chunk_cumsum(sum_mat, x):
    hi = x.astype(BF16)
    r1 = x - hi.astype(F32)
    mid = r1.astype(BF16)
    lo = (r1 - mid.astype(F32)).astype(BF16)
    s = jnp.dot(sum_mat, jnp.concatenate([hi, mid, lo], axis=1), preferred_element_type=F32)
    n = x.shape[1]
    return s[:, 0:n] + s[:, n:2 * n] + s[:, 2 * n:3 * n]


def _chunk_row(x, chunk, r, rows_out):
    picks = [jnp.broadcast_to(x[c * chunk + r:c * chunk + r + 1, :], (rows_out, x.shape[1]))
             for c in range(x.shape[0] // chunk)]
    return jnp.concatenate(picks, axis=0)


def _gated_prepare(q, k, la, sum_mat, bufs, rows, dec_rows, *, chunk, forward):
    qt_ref, kt_ref, qh_ref, kh_ref, dec_ref = bufs
    cum = _chunk_cumsum(sum_mat, la)
    ref = _chunk_row(cum, chunk, chunk // 2, chunk)
    tot_row = chunk - 1 if forward else 0
    tot = _chunk_row(cum, chunk, tot_row, chunk)
    qt_ref[rows, :] = (q * jnp.exp(cum - ref)).astype(BF16)
    kt_ref[rows, :] = (k * jnp.exp(ref - cum)).astype(BF16)
    qh_ref[rows, :] = (q * jnp.exp(cum)).astype(BF16)
    kh_ref[rows, :] = (k * jnp.exp(tot - cum)).astype(BF16)
    dec_ref[dec_rows, :] = jnp.exp(_chunk_row(cum, chunk, tot_row, SUBLANE))


def _gated_scan(bufs_f, bufs_b, vb_ref, o_f, o_b, st_f, st_b, *, seq, chunk):
    n_chunks = seq // chunk
    row = lax.broadcasted_iota(jnp.int32, (2 * chunk, chunk), 0) % chunk
    col = lax.broadcasted_iota(jnp.int32, (2 * chunk, chunk), 1)
    lower = row >= col
    upper = row <= col
    head0_lane = lax.broadcasted_iota(jnp.int32, (chunk, LANE), 1) < GLA_DK
    head0_val = lax.broadcasted_iota(jnp.int32, (chunk, 2 * LANE), 1) < LANE
    st_row = lax.broadcasted_iota(jnp.int32, (2 * LANE, LANE), 0) < LANE
    st_col = lax.broadcasted_iota(jnp.int32, (2 * LANE, LANE), 1) < GLA_DK
    st_mask = (st_row == st_col).astype(F32)

    st_f[...] = jnp.zeros_like(st_f)
    st_b[...] = jnp.zeros_like(st_b)

    def scores(c, forward):
        qt_ref, kt_ref = (bufs_f if forward else bufs_b)[0:2]
        rows = pl.ds(pl.multiple_of(c * chunk, chunk), chunk)
        qt = qt_ref[rows, :]
        zero = jnp.zeros_like(qt)
        q2 = jnp.concatenate([jnp.where(head0_lane, qt, zero), jnp.where(head0_lane, zero, qt)], axis=0)
        a = lax.dot_general(q2, kt_ref[rows, :], NT_DIMS, preferred_element_type=F32)
        return jnp.where(lower if forward else upper, a, 0.0).astype(BF16)

    def visit(c, forward, a):
        _, _, qh_ref, kh_ref, dec_ref = bufs_f if forward else bufs_b
        st_ref = st_f if forward else st_b
        o_ref = o_f if forward else o_b
        rows = pl.ds(pl.multiple_of(c * chunk, chunk), chunk)
        vb = vb_ref[rows, :]
        o2 = jnp.dot(a, vb, preferred_element_type=F32)
        o = jnp.where(head0_val, o2[:chunk, :], o2[chunk:, :])
        st = st_ref[...]
        o = o + lax.dot_general(qh_ref[rows, :], st.astype(BF16), NT_DIMS, preferred_element_type=F32)
        ds = lax.dot_general(vb, kh_ref[rows, :], TN_DIMS, preferred_element_type=F32)
        dec = dec_ref[pl.ds(pl.multiple_of(c * SUBLANE, SUBLANE), SUBLANE), :][0:1, :]
        st_ref[...] = st * dec + ds * st_mask
        o_ref[rows, :] = o

    def body(c, carry):
        nxt = jnp.minimum(c + 1, n_chunks - 1)
        ahead = (scores(nxt, True), scores(n_chunks - 1 - nxt, False))
        visit(c, True, carry[0])
        visit(n_chunks - 1 - c, False, carry[1])
        return ahead

    lax.fori_loop(0, n_chunks, body, (scores(0, True), scores(n_chunks - 1, False)), unroll=2)


def _gated_mixer_body(block_inputs, v_ref, gate_ref, g_ref, y_ref, scratch, *, seq, chunk):
    bufs_f, bufs_b, (vb_ref, o_f, o_b, st_f, st_b) = scratch[0:5], scratch[5:10], scratch[10:]
    block = min(GATED_BLOCK, seq)
    n_blocks = seq // block
    dec_per_block = block // chunk * SUBLANE
    sum_f = _decay_sum_matrix(block, chunk, True)
    sum_b = _decay_sum_matrix(block, chunk, False)

    def prepare(i, carry):
        rows = pl.ds(pl.multiple_of(i * block, block), block)
        dec_rows = pl.ds(pl.multiple_of(i * dec_per_block, dec_per_block), dec_per_block)
        q, k_f, k_b, la_f, la_b = block_inputs(rows)
        _gated_prepare(q, k_f, la_f, sum_f, bufs_f, rows, dec_rows, chunk=chunk, forward=True)
        _gated_prepare(q, k_b, la_b, sum_b, bufs_b, rows, dec_rows, chunk=chunk, forward=False)
        vb_ref[rows, :] = v_ref[rows, :].astype(BF16)
        return carry

    lax.fori_loop(0, n_blocks, prepare, 0)
    _gated_scan(bufs_f, bufs_b, vb_ref, o_f, o_b, st_f, st_b, seq=seq, chunk=chunk)

    g = g_ref[...]

    def finish(i, carry):
        rows = pl.ds(pl.multiple_of(i * block, block), block)
        o = o_f[rows, :] + o_b[rows, :]
        gate = gate_ref[rows, :]
        outs = []
        for h in range(2):
            cols = slice(h * LANE, (h + 1) * LANE)
            outs.append(_rms(o[:, cols], g[:, cols]) * _silu(gate[:, cols]))
        y_ref[rows, :] = jnp.concatenate(outs, axis=1).astype(y_ref.dtype)
        return carry

    lax.fori_loop(0, n_blocks, finish, 0)


def _gla_pair_kernel(q_ref, k_ref, v_ref, r_ref, lr_ref, wgf_ref, wgb_ref, baf_ref, bab_ref, g_ref,
                     y_ref, *scratch, seq, chunk):
    def block_inputs(rows):
        lr = lr_ref[rows, :].astype(BF16)
        las = []
        for w_ref, b_ref in ((wgf_ref, baf_ref), (wgb_ref, bab_ref)):
            z = jnp.dot(lr, w_ref[...], preferred_element_type=F32) + b_ref[...]
            las.append(_log_sigmoid(z) * (1.0 / GLA_GATE_NORM))
        k = k_ref[rows, :] * (GLA_DK ** -0.5)
        return q_ref[rows, :], k, k, las[0], las[1]

    _gated_mixer_body(block_inputs, v_ref, r_ref, g_ref, y_ref, scratch, seq=seq, chunk=chunk)


def _hgrn_pair_kernel(q_ref, zf_ref, zb_ref, v_ref, gate_ref, lbf_ref, lbb_ref, g_ref,
                      y_ref, *scratch, seq, chunk):
    def block_inputs(rows):
        ks, las = [], []
        for z_ref, lb_ref in ((zf_ref, lbf_ref), (zb_ref, lbb_ref)):
            z = z_ref[rows, :]
            lb = lb_ref[...]
            f = lb + (1.0 - lb) * _sigmoid(z)
            las.append(jnp.log(jnp.maximum(f, HGRN_MIN_F)))
            ks.append((1.0 - lb) * _sigmoid(-z))
        return _silu(q_ref[rows, :]), ks[0], ks[1], las[0], las[1]

    _gated_mixer_body(block_inputs, v_ref, gate_ref, g_ref, y_ref, scratch, seq=seq, chunk=chunk)


def _col_spec(seq, width, block_fn):
    return pl.BlockSpec((None, seq, width), lambda b, p: (b, 0, block_fn(p)))


def _pair_spec(shape):
    return pl.BlockSpec((None,) + shape, lambda b, p: (p,) + (0,) * len(shape))


def _gated_scratch(seq, chunk):
    per_dir = [pltpu.VMEM((seq, LANE), BF16) for _ in range(4)]
    per_dir.append(pltpu.VMEM((seq // chunk * SUBLANE, LANE), F32))
    return per_dir * 2 + [pltpu.VMEM((seq, 2 * LANE), BF16),
                          pltpu.VMEM((seq, 2 * LANE), F32), pltpu.VMEM((seq, 2 * LANE), F32),
                          pltpu.VMEM((2 * LANE, LANE), F32), pltpu.VMEM((2 * LANE, LANE), F32)]


EVEN_BLK = dict(gq=0, gk=2, gv=4, gr=8, hq=12, hzf=14, hzb=16, hi=18, hg=22, lr=26)
EVEN_COLS_PADDED = 27 * LANE


def _gla_mixer(p3, wgf, wgb, baf, bab, norm_g):
    b, seq, _ = p3.shape
    blk = EVEN_BLK
    kern = functools.partial(_gla_pair_kernel, seq=seq, chunk=GATED_CHUNK)
    return pl.pallas_call(
        kern, grid=(b, 2),
        in_specs=[_col_spec(seq, LANE, lambda p: blk["gq"] + p),
                  _col_spec(seq, LANE, lambda p: blk["gk"] + p),
                  _col_spec(seq, 2 * LANE, lambda p: blk["gv"] // 2 + p),
                  _col_spec(seq, 2 * LANE, lambda p: blk["gr"] // 2 + p),
                  _col_spec(seq, LANE, lambda p: blk["lr"]),
                  _pair_spec((LANE, LANE)), _pair_spec((LANE, LANE)),
                  _pair_spec((1, LANE)), _pair_spec((1, LANE)), _pair_spec((1, 2 * LANE))],
        out_specs=pl.BlockSpec((None, seq, 2 * LANE), lambda b, p: (b, 0, p)),
        out_shape=jax.ShapeDtypeStruct((b, seq, GLA_HEADS * GLA_DV), BF16),
        scratch_shapes=_gated_scratch(seq, GATED_CHUNK),
        compiler_params=_cparams(2),
        name="gla_mixer",
    )(p3, p3, p3, p3, p3, wgf, wgb, baf, bab, norm_g)


def _hgrn_mixer(p3, lbf, lbb, norm_g):
    b, seq, _ = p3.shape
    blk = EVEN_BLK
    kern = functools.partial(_hgrn_pair_kernel, seq=seq, chunk=GATED_CHUNK)
    return pl.pallas_call(
        kern, grid=(b, 2),
        in_specs=[_col_spec(seq, LANE, lambda p: blk["hq"] + p),
                  _col_spec(seq, LANE, lambda p: blk["hzf"] + p),
                  _col_spec(seq, LANE, lambda p: blk["hzb"] + p),
                  _col_spec(seq, 2 * LANE, lambda p: blk["hi"] // 2 + p),
                  _col_spec(seq, 2 * LANE, lambda p: blk["hg"] // 2 + p),
                  _pair_spec((1, LANE)), _pair_spec((1, LANE)), _pair_spec((1, 2 * LANE))],
        out_specs=pl.BlockSpec((None, seq, 2 * LANE), lambda b, p: (b, 0, p)),
        out_shape=jax.ShapeDtypeStruct((b, seq, HGRN_HEADS * HGRN_DV), BF16),
        scratch_shapes=_gated_scratch(seq, GATED_CHUNK),
        compiler_params=_cparams(2),
        name="hgrn2_mixer",
    )(p3, p3, p3, p3, p3, lbf, lbb, norm_g)


def _retention_kernel(q_ref, k_ref, v_ref, gate_ref, cos_ref, sin_ref, lg_ref, g_ref,
                      y_ref, qr, kr, qf, kf, qb, kb, vb, a_buf, o_f, o_b, s_f, s_b, *, seq, chunk):
    half = RET_DK // 2
    lg_f = lg_ref[0:1, :]
    lg_b = lg_ref[1:2, :]
    n_chunks = seq // chunk
    row = lax.broadcasted_iota(jnp.int32, (chunk, chunk), 0)
    col = lax.broadcasted_iota(jnp.int32, (chunk, chunk), 1)
    diff = (row - col).astype(F32)
    dmat = (jnp.where(row >= col, jnp.exp(lg_f * diff), 0.0)
            + jnp.where(row <= col, jnp.exp(-lg_b * diff), 0.0))
    pos = lax.broadcasted_iota(jnp.int32, (chunk, LANE), 0).astype(F32)
    lgf = lg_f[:, 0:LANE]
    lgb = lg_b[:, 0:LANE]
    qdec_f = jnp.exp(lgf * (pos + 1.0))
    kdec_f = jnp.exp(lgf * (chunk - 1.0 - pos))
    qdec_b = jnp.exp(lgb * (chunk - pos))
    kdec_b = jnp.exp(lgb * pos)
    tot_f = jnp.exp(lgf[:, 0:1] * float(chunk))
    tot_b = jnp.exp(lgb[:, 0:1] * float(chunk))

    def chunk_rows(c):
        return pl.ds(pl.multiple_of(c * chunk, chunk), chunk)

    def prepare(c, carry):
        rows = chunk_rows(c)
        cos = cos_ref[rows, :]
        sin = sin_ref[rows, :]
        q = q_ref[rows, :]
        k = k_ref[rows, :]
        q_rot = q * cos + pltpu.roll(q, half, 1) * sin
        k_rot = (k * cos + pltpu.roll(k, half, 1) * sin) * (RET_DK ** -0.5)
        qr[rows, :] = q_rot.astype(BF16)
        kr[rows, :] = k_rot.astype(BF16)
        qf[rows, :] = (q_rot * qdec_f).astype(BF16)
        kf[rows, :] = (k_rot * kdec_f).astype(BF16)
        qb[rows, :] = (q_rot * qdec_b).astype(BF16)
        kb[rows, :] = (k_rot * kdec_b).astype(BF16)
        vb[rows, :] = v_ref[rows, :].astype(BF16)
        return carry

    lax.fori_loop(0, n_chunks, prepare, 0)

    def scores(c):
        rows = chunk_rows(c)
        a = lax.dot_general(qr[rows, :], kr[rows, :], NT_DIMS, preferred_element_type=F32)
        return (a * dmat).astype(BF16)

    s_f[...] = jnp.zeros_like(s_f)
    s_b[...] = jnp.zeros_like(s_b)
    a_buf[0] = scores(0)

    def visit(c, slot):
        a_buf[1 - slot] = scores(jnp.minimum(c + 1, n_chunks - 1))
        rows = chunk_rows(c)
        vc = vb[rows, :]
        s_old = s_f[...]
        o = jnp.dot(a_buf[slot], vc, preferred_element_type=F32)
        o_f[rows, :] = o + jnp.dot(qf[rows, :], s_old.astype(BF16), preferred_element_type=F32)
        s_f[...] = s_old * tot_f + lax.dot_general(kf[rows, :], vc, TN_DIMS, preferred_element_type=F32)

        rows_b = chunk_rows(n_chunks - 1 - c)
        vc_b = vb[rows_b, :]
        s_old_b = s_b[...]
        o_b[rows_b, :] = jnp.dot(qb[rows_b, :], s_old_b.astype(BF16), preferred_element_type=F32)
        s_b[...] = s_old_b * tot_b + lax.dot_general(kb[rows_b, :], vc_b, TN_DIMS,
                                                     preferred_element_type=F32)

    def body(i, carry):
        visit(2 * i, 0)
        visit(2 * i + 1, 1)
        return carry

    lax.fori_loop(0, n_chunks // 2, body, 0)

    g = g_ref[...]
    vlane = lax.broadcasted_iota(jnp.int32, (1, RET_DV_PAD), 1) < RET_DV

    def finish(c, carry):
        rows = chunk_rows(c)
        o = o_f[rows, :] + o_b[rows, :]
        mu = jnp.sum(o, axis=-1, keepdims=True) * (1.0 / RET_DV)
        cen = jnp.where(vlane, o - mu, 0.0)
        var = jnp.sum(cen * cen, axis=-1, keepdims=True) * (1.0 / RET_DV)
        y = cen * lax.rsqrt(var + EPS) * g
        y_ref[rows, :] = (y * _silu(gate_ref[rows, :])).astype(y_ref.dtype)
        return carry

    lax.fori_loop(0, n_chunks, finish, 0)


ODD_BLK = dict(rq=0, rk=4, rv=8, rg=16, su=24)
ODD_COLS_PADDED = 26 * LANE


def _retention_mixer(p3, cos2, sin2, log_gamma, norm_g):
    b, seq, _ = p3.shape
    blk = ODD_BLK
    chunk = min(RET_CHUNK, seq)
    kern = functools.partial(_retention_kernel, seq=seq, chunk=chunk)
    head = lambda shape: pl.BlockSpec((None,) + shape, lambda b, h: (h,) + (0,) * len(shape))
    table = pl.BlockSpec((seq, LANE), lambda b, h: (0, 0))
    return pl.pallas_call(
        kern, grid=(b, RET_HEADS),
        in_specs=[pl.BlockSpec((None, seq, LANE), lambda b, h: (b, 0, blk["rq"] + h)),
                  pl.BlockSpec((None, seq, LANE), lambda b, h: (b, 0, blk["rk"] + h)),
                  pl.BlockSpec((None, seq, RET_DV_PAD), lambda b, h: (b, 0, blk["rv"] // 2 + h)),
                  pl.BlockSpec((None, seq, RET_DV_PAD), lambda b, h: (b, 0, blk["rg"] // 2 + h)),
                  table, table, head((2, chunk)), head((1, RET_DV_PAD))],
        out_specs=pl.BlockSpec((None, seq, RET_DV_PAD), lambda b, h: (b, 0, h)),
        out_shape=jax.ShapeDtypeStruct((b, seq, RET_HEADS * RET_DV_PAD), BF16),
        scratch_shapes=([pltpu.VMEM((seq, LANE), BF16) for _ in range(6)]
                        + [pltpu.VMEM((seq, RET_DV_PAD), BF16),
                           pltpu.VMEM((2, chunk, chunk), BF16),
                           pltpu.VMEM((seq, RET_DV_PAD), F32), pltpu.VMEM((seq, RET_DV_PAD), F32),
                           pltpu.VMEM((RET_DK, RET_DV_PAD), F32), pltpu.VMEM((RET_DK, RET_DV_PAD), F32)]),
        compiler_params=_cparams(2),
        name="retention_mixer",
    )(p3, p3, p3, p3, cos2, sin2, log_gamma, norm_g)


def _s5_state_kernel(u_ref, b_ref, lam_ref, h_ref, xs_ref, hs_ref, *, n_rows):
    x = jnp.dot(u_ref[...].astype(BF16), b_ref[...], preferred_element_type=F32)
    n_tiles = 2 * S5_GROUPS
    for t in range(n_tiles):
        xs_ref[t] = x[:, t * LANE:(t + 1) * LANE]
    half = S5_STATE
    row = lax.broadcasted_iota(jnp.int32, (n_rows, LANE), 0)
    lane = lax.broadcasted_iota(jnp.int32, (1, LANE), 1)
    sign = jnp.where(lane < half, -1.0, 1.0)

    def cmul(z, a_re, a_im_signed):
        return z * a_re + pltpu.roll(z, half, 1) * a_im_signed

    def group(g, carry):
        lam = lam_ref[g]
        for d in range(2):
            z = xs_ref[2 * g + d]
            a_re = lam[2 * d:2 * d + 1, :]
            a_im = lam[2 * d + 1:2 * d + 2, :] * sign
            step = 1
            while step < n_rows:
                if d == 0:
                    shifted = jnp.where(row >= step, pltpu.roll(z, step, 0), 0.0)
                else:
                    shifted = jnp.where(row < n_rows - step, pltpu.roll(z, n_rows - step, 0), 0.0)
                z = z + cmul(shifted, a_re, a_im)
                a_re, a_im = a_re * a_re - a_im * a_im, 2.0 * a_re * a_im
                step *= 2
            if d == 0:
                z = jnp.where(row >= 1, pltpu.roll(z, 1, 0), 0.0)
            else:
                z = jnp.where(row < n_rows - 1, pltpu.roll(z, n_rows - 1, 0), 0.0)
            hs_ref[2 * g + d] = z.astype(BF16)
        return carry

    lax.fori_loop(0, S5_GROUPS, group, 0)
    for t in range(n_tiles):
        h_ref[:, t * LANE:(t + 1) * LANE] = hs_ref[t]


def _s5_out_kernel(u_ref, h_ref, mt_ref, ct_ref, y_ref):
    y = lax.dot_general(u_ref[...].astype(BF16), mt_ref[...], NT_DIMS, preferred_element_type=F32)
    y_ref[...] = y + lax.dot_general(h_ref[...], ct_ref[...], NT_DIMS, preferred_element_type=F32)


def _s5_core(u3, m, bst, cst, lam):
    b, n_rows, w = u3.shape
    n_state = bst.shape[1]
    batch = lambda width: pl.BlockSpec((None, n_rows, width), lambda bi: (bi, 0, 0))
    full = lambda a: pl.BlockSpec(a.shape, lambda bi: (0,) * a.ndim)
    h = pl.pallas_call(
        functools.partial(_s5_state_kernel, n_rows=n_rows),
        grid=(b,),
        in_specs=[batch(w), full(bst), full(lam)],
        out_specs=batch(n_state),
        out_shape=jax.ShapeDtypeStruct((b, n_rows, n_state), BF16),
        scratch_shapes=[pltpu.VMEM((n_state // LANE, n_rows, LANE), F32),
                        pltpu.VMEM((n_state // LANE, n_rows, LANE), BF16)],
        compiler_params=_cparams(1),
        name="s5_state",
    )(u3, bst, lam)
    return pl.pallas_call(
        _s5_out_kernel, grid=(b,),
        in_specs=[batch(w), batch(n_state), full(m), full(cst)],
        out_specs=batch(w),
        out_shape=jax.ShapeDtypeStruct((b, n_rows, w), F32),
        compiler_params=_cparams(1),
        name="s5_out",
    )(u3, h, m, cst)


def _s5_post_kernel(y_ref, u_ref, d_ref, w_ref, b_ref, o_ref):
    y = y_ref[...] + d_ref[...] * u_ref[...]
    g = _gelu_tanh(y)
    z = jnp.dot(g.astype(BF16), w_ref[...], preferred_element_type=F32) + b_ref[...]
    o_ref[...] = (g * _sigmoid(z)).astype(o_ref.dtype)


def _s5_post(y2d, p2d, d_skip, glu_w, glu_b):
    t, w = y2d.shape
    row = lambda i: (i, 0)
    const = lambda i: (0, 0)
    return pl.pallas_call(
        _s5_post_kernel, grid=(t // ROW_TILE,),
        in_specs=[pl.BlockSpec((ROW_TILE, w), row),
                  pl.BlockSpec((ROW_TILE, w), lambda i: (i, ODD_BLK["su"] // 2)),
                  pl.BlockSpec((1, w), const), pl.BlockSpec((w, w), const), pl.BlockSpec((1, w), const)],
        out_specs=pl.BlockSpec((ROW_TILE, w), row),
        out_shape=jax.ShapeDtypeStruct((t, w), BF16),
        compiler_params=_cparams(1),
        name="s5_glu",
    )(y2d, p2d, d_skip.reshape(1, w), glu_w.astype(BF16), glu_b.reshape(1, w))


def _s5_operators(lam_re, lam_im, log_dt, b_re, b_im, c_re, c_im):
    L = S5_CHUNK
    steps = jnp.arange(L, dtype=F32)
    cre, cim = c_re.astype(F32), c_im.astype(F32)
    c_stack = jnp.concatenate([cre, cim], axis=-1)

    t_sum = 0.0
    bst_parts, cst_parts, lam_rows = [], [], []
    for d in range(2):
        lr = jnp.minimum(lam_re[d].astype(F32), -1e-4)
        li = lam_im[d].astype(F32)
        dt = jnp.exp(log_dt[d].astype(F32))[:, None]
        a, w = lr * dt, li * dt
        mag = jnp.exp(a)
        ar, ai = mag * jnp.cos(w), mag * jnp.sin(w)
        den = lr * lr + li * li
        nr = ar - 1.0
        cr = (nr * lr + ai * li) / den
        ci = (ai * lr - nr * li) / den
        br, bi = b_re.astype(F32).transpose(0, 2, 1), b_im.astype(F32).transpose(0, 2, 1)
        bbr = cr[:, None, :] * br - ci[:, None, :] * bi
        bbi = cr[:, None, :] * bi + ci[:, None, :] * br

        def power(lag, a=a, w=w):
            m_ = jnp.exp(a * lag)
            return m_ * jnp.cos(w * lag), m_ * jnp.sin(w * lag)

        jj = steps[None, :, None, None, None]
        ii = steps[None, None, None, :, None]
        lag = jnp.maximum(ii - jj, 0.0) if d == 0 else jnp.maximum(jj - ii, 0.0)
        valid = (ii >= jj) if d == 0 else (jj >= ii)
        a5, w5 = a[:, None, None, None, :], w[:, None, None, None, :]
        pr, pi = power(lag, a5, w5)
        b5r, b5i = bbr[:, None, :, None, :], bbi[:, None, :, None, :]
        sr = jnp.where(valid, pr * b5r - pi * b5i, 0.0)
        si = jnp.where(valid, pr * b5i + pi * b5r, 0.0)
        t_sum = t_sum + jnp.concatenate([sr, -si], axis=-1)

        lag_s = (L - 1.0 - steps) if d == 0 else steps
        pr, pi = power(lag_s[None, :, None, None], a[:, None, None, :], w[:, None, None, :])
        bst_parts += [pr * bbr[:, None] - pi * bbi[:, None], pr * bbi[:, None] + pi * bbr[:, None]]

        lag_c = (steps + 1.0) if d == 0 else (L - steps)
        pr, pi = power(lag_c[None, :, None, None], a[:, None, None, :], w[:, None, None, :])
        cst_parts += [cre[:, None] * pr - cim[:, None] * pi,
                      -cre[:, None] * pi - cim[:, None] * pr]

        pr_l, pi_l = power(float(L))
        lam_rows += [jnp.concatenate([pr_l, pr_l], -1), jnp.concatenate([pi_l, pi_l], -1)]

    m = jnp.einsum('gjpin,gqn->gjpiq', t_sum, c_stack, precision=HIGHEST)
    m = m.reshape(S5_GROUPS, L, S5_GROUP_CH, L * S5_GROUP_CH)
    bst = jnp.stack(bst_parts, axis=3).reshape(S5_GROUPS, L, S5_GROUP_CH, 4 * S5_STATE)
    cst = jnp.stack(cst_parts, axis=3).reshape(S5_GROUPS, L, S5_GROUP_CH, 4 * S5_STATE)
    lam = jnp.stack(lam_rows, 1)

    def embed(cat):
        g, l, p, w_ = cat.shape
        flat = cat.reshape(g * l * p, w_).astype(BF16)
        row_g = lax.broadcasted_iota(jnp.int32, (g * l * p, 1), 0) // (l * p)
        blocks = jnp.concatenate([jnp.where(row_g == h, flat, 0) for h in range(g)], axis=1)
        return blocks.reshape(g, l, p, g * w_).transpose(1, 0, 2, 3).reshape(l * g * p, g * w_)

    b_full = embed(bst)
    c_full_t = embed(cst)
    m_rows = embed(m)
    m_full_t = m_rows.T.reshape(S5_GROUPS, L, S5_GROUP_CH, L * S5_WIDTH)
    m_full_t = m_full_t.transpose(1, 0, 2, 3).reshape(L * S5_WIDTH, L * S5_WIDTH)
    return m_full_t.astype(BF16), b_full.astype(BF16), c_full_t.astype(BF16), lam


def _proj_ffn_kernel(x_ref, xp_ref, xn_ref, ya_ref, yap_ref, yan_ref, yb_ref, ybp_ref, ybn_ref,
                     wa_ref, wb_ref, g_ref, wup_ref, cw_ref, cb_ref, wdn_ref, gfin_ref,
                     o_ref, act_ref, *, tiles_per_seq, final_norm):
    tm = x_ref.shape[0]
    ext = tm + 2 * HALO
    i = pl.program_id(0)
    first = (i % tiles_per_seq) == 0
    last = (i % tiles_per_seq) == tiles_per_seq - 1
    rows3 = lambda p, m, n: jnp.concatenate([p[...], m[...], n[...]], axis=0)
    x1 = rows3(xp_ref, x_ref, xn_ref)
    x1 = x1 + jnp.dot(rows3(yap_ref, ya_ref, yan_ref), wa_ref[...], preferred_element_type=F32)
    x1 = x1 + jnp.dot(rows3(ybp_ref, yb_ref, ybn_ref), wb_ref[...], preferred_element_type=F32)
    r = lax.broadcasted_iota(jnp.int32, (ext, 1), 0)
    outside = ((r < HALO) & first) | ((r >= tm + HALO) & last)
    h = jnp.where(outside, 0.0, _rms(x1, g_ref[...])).astype(BF16)
    down = []
    for j0 in range(0, FFN_DIM, FFN_GROUP):
        width = min(FFN_GROUP, FFN_DIM - j0)
        for j in range(j0, j0 + width, FFN_CHUNK):
            gated = []
            for base in (0, FFN_DIM):
                cols = slice(base + j, base + j + FFN_CHUNK)
                u = jnp.dot(h, wup_ref[:, cols], preferred_element_type=F32)
                u_prev = pltpu.roll(u, 1, 0)[HALO:HALO + tm, :]
                u_next = pltpu.roll(u, ext - 1, 0)[HALO:HALO + tm, :]
                c = cb_ref[:, cols] + u_prev * cw_ref[0:1, cols]
                c = c + u[HALO:HALO + tm, :] * cw_ref[1:2, cols]
                c = c + u_next * cw_ref[2:3, cols]
                gated.append(c)
            act_ref[:, j - j0:j - j0 + FFN_CHUNK] = (_silu(gated[0]) * gated[1]).astype(BF16)
        down.append(jnp.dot(act_ref[:, 0:width], wdn_ref[j0:j0 + width, :], preferred_element_type=F32))
    out = x1[HALO:HALO + tm, :] + functools.reduce(lambda a, b: a + b, down)
    if final_norm:
        out = _rms(out, gfin_ref[...])
    o_ref[...] = out


def _proj_ffn(x2d, seq, ya, yb, wa, wb, g, w_up, conv_w, conv_b, w_down, g_final, final_norm):
    t, d = x2d.shape
    tm = min(FFN_ROW_TILE, seq)
    tiles_per_seq = seq // tm
    halos_per_tile = tm // HALO
    n_halo_blocks = t // HALO

    def tile3(width):
        return [pl.BlockSpec((tm, width), lambda i: (i, 0)),
                pl.BlockSpec((HALO, width), lambda i: (jnp.maximum(i * halos_per_tile - 1, 0), 0)),
                pl.BlockSpec((HALO, width),
                             lambda i: (jnp.minimum((i + 1) * halos_per_tile, n_halo_blocks - 1), 0))]

    def resident(a):
        return pl.BlockSpec(a.shape, lambda i: (0,) * a.ndim)

    kern = functools.partial(_proj_ffn_kernel, tiles_per_seq=tiles_per_seq, final_norm=final_norm)
    consts = [wa, wb, g.reshape(1, d), w_up.astype(BF16), conv_w, conv_b.reshape(1, 2 * FFN_DIM),
              w_down.astype(BF16), g_final.reshape(1, d)]
    return pl.pallas_call(
        kern, grid=(t // tm,),
        in_specs=tile3(d) + tile3(ya.shape[1]) + tile3(yb.shape[1]) + [resident(c) for c in consts],
        out_specs=pl.BlockSpec((tm, d), lambda i: (i, 0)),
        out_shape=jax.ShapeDtypeStruct((t, d), F32),
        scratch_shapes=[pltpu.VMEM((tm, FFN_GROUP), BF16)],
        compiler_params=_cparams(1),
        name="proj_conv_ffn",
    )(x2d, x2d, x2d, ya, ya, ya, yb, yb, yb, *consts)


def _pad_cols(w, width):
    return jnp.pad(w, ((0, 0), (0, width - w.shape[1])))


def _even_layer_mix(x2d, b, seq, norm_g, w_in, w_out, wa2, ba, gla_g, lb_f, lb_b, hgrn_g):
    sizes = (256, 256, 512, 512, 16, 16, 256, 256, 256, 512, 512)
    offs = np.concatenate([[0], np.cumsum(sizes)])
    gq, gk, gv, gr, glf, glb, hq, hzf, hzb, hi, hg = (slice(offs[i], offs[i + 1]) for i in range(11))
    w_perm = jnp.concatenate([w_in[:, s] for s in (gq, gk, gv, gr, hq, hzf, hzb, hi, hg, glf, glb)], axis=1)
    w_perm = _pad_cols(w_perm, EVEN_COLS_PADDED).astype(BF16)
    p = _norm_matmul(x2d, norm_g, w_perm).reshape(b, seq, EVEN_COLS_PADDED)

    def gate_w(direction):
        w = jnp.zeros((LANE, GLA_HEADS * GLA_DK), F32)
        w = w.at[direction * GLA_RANK:(direction + 1) * GLA_RANK, :].set(wa2[direction].astype(F32))
        return w.reshape(LANE, 2, LANE).transpose(1, 0, 2).astype(BF16)

    baf = ba[0].astype(F32).reshape(2, 1, LANE)
    bab = ba[1].astype(F32).reshape(2, 1, LANE)
    ya = _gla_mixer(p, gate_w(0), gate_w(1), baf, bab, gla_g.astype(F32).reshape(2, 1, 2 * LANE))
    yb = _hgrn_mixer(p, lb_f.reshape(2, 1, LANE), lb_b.reshape(2, 1, LANE),
                     hgrn_g.astype(F32).reshape(2, 1, 2 * LANE))
    t = b * seq
    n_a = GLA_HEADS * GLA_DV
    return ya.reshape(t, n_a), yb.reshape(t, -1), w_out[:n_a].astype(BF16), w_out[n_a:].astype(BF16)


def _pad_heads(w, axis):
    shape = list(w.shape)
    shape[axis:axis + 1] = [RET_HEADS, RET_DV]
    w = w.reshape(shape)
    pad = [(0, 0)] * w.ndim
    pad[axis + 1] = (0, RET_DV_PAD - RET_DV)
    w = jnp.pad(w, pad)
    shape[axis:axis + 2] = [RET_HEADS * RET_DV_PAD]
    return w.reshape(shape)


def _odd_layer_mix(x2d, b, seq, norm_g, w_in, w_out, ret_g, lam_re, lam_im, log_dt, b_re, b_im,
                   c_re, c_im, d_skip, glu_w, glu_b, rope, log_gamma):
    hk = RET_HEADS * RET_DK
    hv = RET_HEADS * RET_DV
    w_perm = jnp.concatenate([w_in[:, :2 * hk],
                              _pad_heads(w_in[:, 2 * hk:2 * hk + hv], 1),
                              _pad_heads(w_in[:, 2 * hk + hv:2 * hk + 2 * hv], 1),
                              w_in[:, 2 * hk + 2 * hv:]], axis=1).astype(BF16)
    p2d = _norm_matmul(x2d, norm_g, w_perm)
    p = p2d.reshape(b, seq, ODD_COLS_PADDED)
    g_pad = _pad_heads(ret_g.astype(F32), 0).reshape(RET_HEADS, 1, RET_DV_PAD)
    yc = _retention_mixer(p, rope[0], rope[1], log_gamma, g_pad)

    su = p2d[:, ODD_BLK["su"] * LANE:]
    u3 = su.reshape(b, seq // S5_CHUNK, S5_CHUNK * S5_WIDTH)
    m, bst, cst, lam = _s5_operators(lam_re, lam_im, log_dt, b_re, b_im, c_re, c_im)
    y2d = _s5_core(u3, m, bst, cst, lam).reshape(b * seq, S5_WIDTH)
    yd = _s5_post(y2d, p2d, d_skip.astype(F32), glu_w, glu_b.astype(F32))

    t = b * seq
    w_ret = _pad_heads(w_out[:hv], 0).astype(BF16)
    return yc.reshape(t, -1), yd, w_ret, w_out[hv:].astype(BF16)


def _rope_tables(seq):
    half = RET_DK // 2
    inv = ROPE_BASE ** (-jnp.arange(half, dtype=F32) / half)
    ang = jnp.arange(seq, dtype=F32)[:, None] * inv[None, :]
    cos, sin = jnp.cos(ang), jnp.sin(ang)
    return jnp.concatenate([cos, cos], -1), jnp.concatenate([-sin, sin], -1)


def _retention_log_gamma(chunk):
    hidx = jnp.arange(RET_HEADS, dtype=F32)
    lg = jnp.stack([jnp.log1p(-jnp.exp2(-5.0 - hidx)), jnp.log1p(-jnp.exp2(-5.5 - hidx))], axis=1)
    return jnp.broadcast_to(lg[:, :, None], (RET_HEADS, 2, chunk))


def _hgrn_lower_bounds(lb_logits):
    p = jax.nn.softmax(lb_logits.astype(F32), axis=1)
    return jnp.cumsum(p, axis=1) - p[:, :1]


def kernel(x, mix_norm_g, ffn_norm_g, final_norm_g, w_in_even, w_out_even, gla_wa2, gla_ba, gla_norm_g,
           hgrn_lb_logits, hgrn_norm_g, w_in_odd, w_out_odd, ret_norm_g, s5_lam_re, s5_lam_im, s5_log_dt,
           s5_b_re, s5_b_im, s5_c_re, s5_c_im, s5_d, s5_glu_w, s5_glu_b,
           ffn_w_up, ffn_conv_w, ffn_conv_b, ffn_w_down):
    b, seq, d = x.shape
    lbs = _hgrn_lower_bounds(hgrn_lb_logits)
    rope = _rope_tables(seq)
    log_gamma = _retention_log_gamma(min(RET_CHUNK, seq))
    x2d = x.reshape(b * seq, d)
    for layer in range(DEPTH):
        j = layer // 2
        if layer % 2 == 0:
            mix = _even_layer_mix(x2d, b, seq, mix_norm_g[layer], w_in_even[j], w_out_even[j], gla_wa2[j],
                                  gla_ba[j], gla_norm_g[j], lbs[0, j], lbs[1, j], hgrn_norm_g[j])
        else:
            mix = _odd_layer_mix(x2d, b, seq, mix_norm_g[layer], w_in_odd[j], w_out_odd[j], ret_norm_g[j],
                                 s5_lam_re[j], s5_lam_im[j], s5_log_dt[j], s5_b_re[j], s5_b_im[j],
                                 s5_c_re[j], s5_c_im[j], s5_d[j], s5_glu_w[j], s5_glu_b[j], rope, log_gamma)
        x2d = _proj_ffn(x2d, seq, *mix, ffn_norm_g[layer], ffn_w_up[layer], ffn_conv_w[layer],
                        ffn_conv_b[layer], ffn_w_down[layer], final_norm_g, final_norm=(layer == DEPTH - 1))
    return x2d.reshape(b, seq, d)
```

```python
import functools
import math

import jax
import jax.numpy as jnp
import numpy as np
from jax import lax
from jax.experimental import pallas as pl
from jax.experimental.pallas import tpu as pltpu

F32 = jnp.float32
BF16 = jnp.bfloat16
HIGHEST = lax.Precision.HIGHEST

D_MODEL = 1024
DEPTH = 4
EPS = 1e-6

GLA_HEADS = 4
GLA_DK = 64
GLA_DV = 128
GLA_RANK = 16
GLA_GATE_NORM = 16.0
HGRN_HEADS = 4
HGRN_DK = 64
HGRN_DV = 128
HGRN_MIN_F = 1e-20

RET_HEADS = 4
RET_DK = 128
RET_DV = 192
RET_DV_PAD = 256
ROPE_BASE = 10000.0

S5_WIDTH = 256
S5_GROUP_CH = 16
S5_GROUPS = 16
S5_STATE = 64
S5_CHUNK = 8

FFN_DIM = 2816
FFN_CHUNK = 256
FFN_GROUP = 4 * FFN_CHUNK
CONV_WIDTH = 3

LANE = 128
SUBLANE = 8
HALO = 2 * SUBLANE
GATED_CHUNK = 64
GATED_BLOCK = 256
RET_CHUNK = 256
ROW_TILE = 512
FFN_ROW_TILE = 1024
VMEM_LIMIT = 60 * 1024 * 1024

NT_DIMS = (((1,), (1,)), ((), ()))
TN_DIMS = (((0,), (0,)), ((), ()))


def _cparams(n_axes):
    return pltpu.CompilerParams(dimension_semantics=("arbitrary",) * n_axes,
                                vmem_limit_bytes=VMEM_LIMIT)


def _rms(x, g):
    return x * lax.rsqrt(jnp.mean(x * x, axis=-1, keepdims=True) + EPS) * g


def _sigmoid(x):
    return 1.0 / (1.0 + jnp.exp(-x))


def _silu(x):
    return x * _sigmoid(x)


def _log_sigmoid(z):
    return jnp.minimum(z, 0.0) - jnp.log(1.0 + jnp.exp(-jnp.abs(z)))


def _gelu_tanh(x):
    c = math.sqrt(2.0 / math.pi)
    return 0.5 * x * (1.0 + jnp.tanh(c * (x + 0.044715 * (x * x * x))))


def _norm_matmul_kernel(x_ref, g_ref, w_ref, o_ref, *, col_chunk):
    h = _rms(x_ref[...], g_ref[...]).astype(BF16)
    n_out = o_ref.shape[1]
    for j in range(n_out // col_chunk):
        cols = slice(j * col_chunk, (j + 1) * col_chunk)
        o_ref[:, cols] = jnp.dot(h, w_ref[:, cols], preferred_element_type=F32)


def _norm_matmul(x2d, g, w):
    t, d = x2d.shape
    e = w.shape[1]
    col_chunk = next(c * LANE for c in (4, 3, 2, 1) if e % (c * LANE) == 0)
    return pl.pallas_call(
        functools.partial(_norm_matmul_kernel, col_chunk=col_chunk),
        grid=(t // ROW_TILE,),
        in_specs=[pl.BlockSpec((ROW_TILE, d), lambda i: (i, 0)),
                  pl.BlockSpec((1, d), lambda i: (0, 0)),
                  pl.BlockSpec((d, e), lambda i: (0, 0))],
        out_specs=pl.BlockSpec((ROW_TILE, e), lambda i: (i, 0)),
        out_shape=jax.ShapeDtypeStruct((t, e), F32),
        compiler_params=_cparams(1),
        name="norm_in_proj",
    )(x2d, g.reshape(1, d), w)


def _decay_sum_matrix(block, chunk, forward):
    i = lax.broadcasted_iota(jnp.int32, (block, block), 0)
    t = lax.broadcasted_iota(jnp.int32, (block, block), 1)
    same = (i // chunk) == (t // chunk)
    return (same & ((t <= i) if forward else (t >= i))).astype(BF16)


def _chunk_cumsum(sum_mat, x):
    hi = x.astype(BF16)
    r1 = x - hi.astype(F32)
    mid = r1.astype(BF16)
    lo = (r1 - mid.astype(F32)).astype(BF16)
    s = jnp.dot(sum_mat, jnp.concatenate([hi, mid, lo], axis=1), preferred_element_type=F32)
    n = x.shape[1]
    return s[:, 0:n] + s[:, n:2 * n] + s[:, 2 * n:3 * n]


def _chunk_row(x, chunk, r, rows_out):
    picks = [jnp.broadcast_to(x[c * chunk + r:c * chunk + r + 1, :], (rows_out, x.shape[1]))
             for c in range(x.shape[0] // chunk)]
    return jnp.concatenate(picks, axis=0)


def _gated_prepare(q, k, la, sum_mat, bufs, rows, dec_rows, *, chunk, forward):
    qt_ref, kt_ref, qh_ref, kh_ref, dec_ref = bufs
    cum = _chunk_cumsum(sum_mat, la)
    ref = _chunk_row(cum, chunk, chunk // 2, chunk)
    tot_row = chunk - 1 if forward else 0
    tot = _chunk_row(cum, chunk, tot_row, chunk)
    qt_ref[rows, :] = (q * jnp.exp(cum - ref)).astype(BF16)
    kt_ref[rows, :] = (k * jnp.exp(ref - cum)).astype(BF16)
    qh_ref[rows, :] = (q * jnp.exp(cum)).astype(BF16)
    kh_ref[rows, :] = (k * jnp.exp(tot - cum)).astype(BF16)
    dec_ref[dec_rows, :] = jnp.exp(_chunk_row(cum, chunk, tot_row, SUBLANE))


def _gated_scan(bufs_f, bufs_b, vb_ref, o_f, o_b, st_f, st_b, *, seq, chunk):
    n_chunks = seq // chunk
    row = lax.broadcasted_iota(jnp.int32, (2 * chunk, chunk), 0) % chunk
    col = lax.broadcasted_iota(jnp.int32, (2 * chunk, chunk), 1)
    lower = row >= col
    upper = row <= col
    head0_lane = lax.broadcasted_iota(jnp.int32, (chunk, LANE), 1) < GLA_DK
    head0_val = lax.broadcasted_iota(jnp.int32, (chunk, 2 * LANE), 1) < LANE
    st_row = lax.broadcasted_iota(jnp.int32, (2 * LANE, LANE), 0) < LANE
    st_col = lax.broadcasted_iota(jnp.int32, (2 * LANE, LANE), 1) < GLA_DK
    st_mask = (st_row == st_col).astype(F32)

    st_f[...] = jnp.zeros_like(st_f)
    st_b[...] = jnp.zeros_like(st_b)

    def scores(c, forward):
        qt_ref, kt_ref = (bufs_f if forward else bufs_b)[0:2]
        rows = pl.ds(pl.multiple_of(c * chunk, chunk), chunk)
        qt = qt_ref[rows, :]
        zero = jnp.zeros_like(qt)
        q2 = jnp.concatenate([jnp.where(head0_lane, qt, zero), jnp.where(head0_lane, zero, qt)], axis=0)
        a = lax.dot_general(q2, kt_ref[rows, :], NT_DIMS, preferred_element_type=F32)
        return jnp.where(lower if forward else upper, a, 0.0).astype(BF16)

    def visit(c, forward, a):
        _, _, qh_ref, kh_ref, dec_ref = bufs_f if forward else bufs_b
        st_ref = st_f if forward else st_b
        o_ref = o_f if forward else o_b
        rows = pl.ds(pl.multiple_of(c * chunk, chunk), chunk)
        vb = vb_ref[rows, :]
        o2 = jnp.dot(a, vb, preferred_element_type=F32)
        o = jnp.where(head0_val, o2[:chunk, :], o2[chunk:, :])
        st = st_ref[...]
        o = o + lax.dot_general(qh_ref[rows, :], st.astype(BF16), NT_DIMS, preferred_element_type=F32)
        ds = lax.dot_general(vb, kh_ref[rows, :], TN_DIMS, preferred_element_type=F32)
        dec = dec_ref[pl.ds(pl.multiple_of(c * SUBLANE, SUBLANE), SUBLANE), :][0:1, :]
        st_ref[...] = st * dec + ds * st_mask
        o_ref[rows, :] = o

    def body(c, carry):
        nxt = jnp.minimum(c + 1, n_chunks - 1)
        ahead = (scores(nxt, True), scores(n_chunks - 1 - nxt, False))
        visit(c, True, carry[0])
        visit(n_chunks - 1 - c, False, carry[1])
        return ahead

    lax.fori_loop(0, n_chunks, body, (scores(0, True), scores(n_chunks - 1, False)), unroll=2)


def _gated_mixer_body(block_inputs, v_ref, gate_ref, g_ref, y_ref, scratch, *, seq, chunk):
    bufs_f, bufs_b, (vb_ref, o_f, o_b, st_f, st_b) = scratch[0:5], scratch[5:10], scratch[10:]
    block = min(GATED_BLOCK, seq)
    n_blocks = seq // block
    dec_per_block = block // chunk * SUBLANE
    sum_f = _decay_sum_matrix(block, chunk, True)
    sum_b = _decay_sum_matrix(block, chunk, False)

    def prepare(i, carry):
        rows = pl.ds(pl.multiple_of(i * block, block), block)
        dec_rows = pl.ds(pl.multiple_of(i * dec_per_block, dec_per_block), dec_per_block)
        q, k_f, k_b, la_f, la_b = block_inputs(rows)
        _gated_prepare(q, k_f, la_f, sum_f, bufs_f, rows, dec_rows, chunk=chunk, forward=True)
        _gated_prepare(q, k_b, la_b, sum_b, bufs_b, rows, dec_rows, chunk=chunk, forward=False)
        vb_ref[rows, :] = v_ref[rows, :].astype(BF16)
        return carry

    lax.fori_loop(0, n_blocks, prepare, 0)
    _gated_scan(bufs_f, bufs_b, vb_ref, o_f, o_b, st_f, st_b, seq=seq, chunk=chunk)

    g = g_ref[...]

    def finish(i, carry):
        rows = pl.ds(pl.multiple_of(i * block, block), block)
        o = o_f[rows, :] + o_b[rows, :]
        gate = gate_ref[rows, :]
        outs = []
        for h in range(2):
            cols = slice(h * LANE, (h + 1) * LANE)
            outs.append(_rms(o[:, cols], g[:, cols]) * _silu(gate[:, cols]))
        y_ref[rows, :] = jnp.concatenate(outs, axis=1).astype(y_ref.dtype)
        return carry

    lax.fori_loop(0, n_blocks, finish, 0)


def _gla_pair_kernel(q_ref, k_ref, v_ref, r_ref, lr_ref, wgf_ref, wgb_ref, baf_ref, bab_ref, g_ref,
                     y_ref, *scratch, seq, chunk):
    def block_inputs(rows):
        lr = lr_ref[rows, :].astype(BF16)
        las = []
        for w_ref, b_ref in ((wgf_ref, baf_ref), (wgb_ref, bab_ref)):
            z = jnp.dot(lr, w_ref[...], preferred_element_type=F32) + b_ref[...]
            las.append(_log_sigmoid(z) * (1.0 / GLA_GATE_NORM))
        k = k_ref[rows, :] * (GLA_DK ** -0.5)
        return q_ref[rows, :], k, k, las[0], las[1]

    _gated_mixer_body(block_inputs, v_ref, r_ref, g_ref, y_ref, scratch, seq=seq, chunk=chunk)


def _hgrn_pair_kernel(q_ref, zf_ref, zb_ref, v_ref, gate_ref, lbf_ref, lbb_ref, g_ref,
                      y_ref, *scratch, seq, chunk):
    def block_inputs(rows):
        ks, las = [], []
        for z_ref, lb_ref in ((zf_ref, lbf_ref), (zb_ref, lbb_ref)):
            z = z_ref[rows, :]
            lb = lb_ref[...]
            f = lb + (1.0 - lb) * _sigmoid(z)
            las.append(jnp.log(jnp.maximum(f, HGRN_MIN_F)))
            ks.append((1.0 - lb) * _sigmoid(-z))
        return _silu(q_ref[rows, :]), ks[0], ks[1], las[0], las[1]

    _gated_mixer_body(block_inputs, v_ref, gate_ref, g_ref, y_ref, scratch, seq=seq, chunk=chunk)


def _col_spec(seq, width, block_fn):
    return pl.BlockSpec((None, seq, width), lambda b, p: (b, 0, block_fn(p)))


def _pair_spec(shape):
    return pl.BlockSpec((None,) + shape, lambda b, p: (p,) + (0,) * len(shape))


def _gated_scratch(seq, chunk):
    per_dir = [pltpu.VMEM((seq, LANE), BF16) for _ in range(4)]
    per_dir.append(pltpu.VMEM((seq // chunk * SUBLANE, LANE), F32))
    return per_dir * 2 + [pltpu.VMEM((seq, 2 * LANE), BF16),
                          pltpu.VMEM((seq, 2 * LANE), F32), pltpu.VMEM((seq, 2 * LANE), F32),
                          pltpu.VMEM((2 * LANE, LANE), F32), pltpu.VMEM((2 * LANE, LANE), F32)]


EVEN_BLK = dict(gq=0, gk=2, gv=4, gr=8, hq=12, hzf=14, hzb=16, hi=18, hg=22, lr=26)
EVEN_COLS_PADDED = 27 * LANE


def _gla_mixer(p3, wgf, wgb, baf, bab, norm_g):
    b, seq, _ = p3.shape
    blk = EVEN_BLK
    kern = functools.partial(_gla_pair_kernel, seq=seq, chunk=GATED_CHUNK)
    return pl.pallas_call(
        kern, grid=(b, 2),
        in_specs=[_col_spec(seq, LANE, lambda p: blk["gq"] + p),
                  _col_spec(seq, LANE, lambda p: blk["gk"] + p),
                  _col_spec(seq, 2 * LANE, lambda p: blk["gv"] // 2 + p),
                  _col_spec(seq, 2 * LANE, lambda p: blk["gr"] // 2 + p),
                  _col_spec(seq, LANE, lambda p: blk["lr"]),
                  _pair_spec((LANE, LANE)), _pair_spec((LANE, LANE)),
                  _pair_spec((1, LANE)), _pair_spec((1, LANE)), _pair_spec((1, 2 * LANE))],
        out_specs=pl.BlockSpec((None, seq, 2 * LANE), lambda b, p: (b, 0, p)),
        out_shape=jax.ShapeDtypeStruct((b, seq, GLA_HEADS * GLA_DV), BF16),
        scratch_shapes=_gated_scratch(seq, GATED_CHUNK),
        compiler_params=_cparams(2),
        name="gla_mixer",
    )(p3, p3, p3, p3, p3, wgf, wgb, baf, bab, norm_g)


def _hgrn_mixer(p3, lbf, lbb, norm_g):
    b, seq, _ = p3.shape
    blk = EVEN_BLK
    kern = functools.partial(_hgrn_pair_kernel, seq=seq, chunk=GATED_CHUNK)
    return pl.pallas_call(
        kern, grid=(b, 2),
        in_specs=[_col_spec(seq, LANE, lambda p: blk["hq"] + p),
                  _col_spec(seq, LANE, lambda p: blk["hzf"] + p),
                  _col_spec(seq, LANE, lambda p: blk["hzb"] + p),
                  _col_spec(seq, 2 * LANE, lambda p: blk["hi"] // 2 + p),
                  _col_spec(seq, 2 * LANE, lambda p: blk["hg"] // 2 + p),
                  _pair_spec((1, LANE)), _pair_spec((1, LANE)), _pair_spec((1, 2 * LANE))],
        out_specs=pl.BlockSpec((None, seq, 2 * LANE), lambda b, p: (b, 0, p)),
        out_shape=jax.ShapeDtypeStruct((b, seq, HGRN_HEADS * HGRN_DV), BF16),
        scratch_shapes=_gated_scratch(seq, GATED_CHUNK),
        compiler_params=_cparams(2),
        name="hgrn2_mixer",
    )(p3, p3, p3, p3, p3, lbf, lbb, norm_g)


def _retention_kernel(q_ref, k_ref, v_ref, gate_ref, cos_ref, sin_ref, lg_ref, g_ref,
                      y_ref, qr, kr, qf, kf, qb, kb, vb, a_buf, o_f, o_b, s_f, s_b, *, seq, chunk):
    half = RET_DK // 2
    lg_f = lg_ref[0:1, :]
    lg_b = lg_ref[1:2, :]
    n_chunks = seq // chunk
    row = lax.broadcasted_iota(jnp.int32, (chunk, chunk), 0)
    col = lax.broadcasted_iota(jnp.int32, (chunk, chunk), 1)
    diff = (row - col).astype(F32)
    dmat = (jnp.where(row >= col, jnp.exp(lg_f * diff), 0.0)
            + jnp.where(row <= col, jnp.exp(-lg_b * diff), 0.0))
    pos = lax.broadcasted_iota(jnp.int32, (chunk, LANE), 0).astype(F32)
    lgf = lg_f[:, 0:LANE]
    lgb = lg_b[:, 0:LANE]
    qdec_f = jnp.exp(lgf * (pos + 1.0))
    kdec_f = jnp.exp(lgf * (chunk - 1.0 - pos))
    qdec_b = jnp.exp(lgb * (chunk - pos))
    kdec_b = jnp.exp(lgb * pos)
    tot_f = jnp.exp(lgf[:, 0:1] * float(chunk))
    tot_b = jnp.exp(lgb[:, 0:1] * float(chunk))

    def chunk_rows(c):
        return pl.ds(pl.multiple_of(c * chunk, chunk), chunk)

    def prepare(c, carry):
        rows = chunk_rows(c)
        cos = cos_ref[rows, :]
        sin = sin_ref[rows, :]
        q = q_ref[rows, :]
        k = k_ref[rows, :]
        q_rot = q * cos + pltpu.roll(q, half, 1) * sin
        k_rot = (k * cos + pltpu.roll(k, half, 1) * sin) * (RET_DK ** -0.5)
        qr[rows, :] = q_rot.astype(BF16)
        kr[rows, :] = k_rot.astype(BF16)
        qf[rows, :] = (q_rot * qdec_f).astype(BF16)
        kf[rows, :] = (k_rot * kdec_f).astype(BF16)
        qb[rows, :] = (q_rot * qdec_b).astype(BF16)
        kb[rows, :] = (k_rot * kdec_b).astype(BF16)
        vb[rows, :] = v_ref[rows, :].astype(BF16)
        return carry

    lax.fori_loop(0, n_chunks, prepare, 0)

    def scores(c):
        rows = chunk_rows(c)
        a = lax.dot_general(qr[rows, :], kr[rows, :], NT_DIMS, preferred_element_type=F32)
        return (a * dmat).astype(BF16)

    s_f[...] = jnp.zeros_like(s_f)
    s_b[...] = jnp.zeros_like(s_b)
    a_buf[0] = scores(0)

    def visit(c, slot):
        a_buf[1 - slot] = scores(jnp.minimum(c + 1, n_chunks - 1))
        rows = chunk_rows(c)
        vc = vb[rows, :]
        s_old = s_f[...]
        o = jnp.dot(a_buf[slot], vc, preferred_element_type=F32)
        o_f[rows, :] = o + jnp.dot(qf[rows, :], s_old.astype(BF16), preferred_element_type=F32)
        s_f[...] = s_old * tot_f + lax.dot_general(kf[rows, :], vc, TN_DIMS, preferred_element_type=F32)

        rows_b = chunk_rows(n_chunks - 1 - c)
        vc_b = vb[rows_b, :]
        s_old_b = s_b[...]
        o_b[rows_b, :] = jnp.dot(qb[rows_b, :], s_old_b.astype(BF16), preferred_element_type=F32)
        s_b[...] = s_old_b * tot_b + lax.dot_general(kb[rows_b, :], vc_b, TN_DIMS,
                                                     preferred_element_type=F32)

    def body(i, carry):
        visit(2 * i, 0)
        visit(2 * i + 1, 1)
        return carry

    lax.fori_loop(0, n_chunks // 2, body, 0)

    g = g_ref[...]
    vlane = lax.broadcasted_iota(jnp.int32, (1, RET_DV_PAD), 1) < RET_DV

    def finish(c, carry):
        rows = chunk_rows(c)
        o = o_f[rows, :] + o_b[rows, :]
        mu = jnp.sum(o, axis=-1, keepdims=True) * (1.0 / RET_DV)
        cen = jnp.where(vlane, o - mu, 0.0)
        var = jnp.sum(cen * cen, axis=-1, keepdims=True) * (1.0 / RET_DV)
        y = cen * lax.rsqrt(var + EPS) * g
        y_ref[rows, :] = (y * _silu(gate_ref[rows, :])).astype(y_ref.dtype)
        return carry

    lax.fori_loop(0, n_chunks, finish, 0)


ODD_BLK = dict(rq=0, rk=4, rv=8, rg=16, su=24)
ODD_COLS_PADDED = 26 * LANE


def _retention_mixer(p3, cos2, sin2, log_gamma, norm_g):
    b, seq, _ = p3.shape
    blk = ODD_BLK
    chunk = min(RET_CHUNK, seq)
    kern = functools.partial(_retention_kernel, seq=seq, chunk=chunk)
    head = lambda shape: pl.BlockSpec((None,) + shape, lambda b, h: (h,) + (0,) * len(shape))
    table = pl.BlockSpec((seq, LANE), lambda b, h: (0, 0))
    return pl.pallas_call(
        kern, grid=(b, RET_HEADS),
        in_specs=[pl.BlockSpec((None, seq, LANE), lambda b, h: (b, 0, blk["rq"] + h)),
                  pl.BlockSpec((None, seq, LANE), lambda b, h: (b, 0, blk["rk"] + h)),
                  pl.BlockSpec((None, seq, RET_DV_PAD), lambda b, h: (b, 0, blk["rv"] // 2 + h)),
                  pl.BlockSpec((None, seq, RET_DV_PAD), lambda b, h: (b, 0, blk["rg"] // 2 + h)),
                  table, table, head((2, chunk)), head((1, RET_DV_PAD))],
        out_specs=pl.BlockSpec((None, seq, RET_DV_PAD), lambda b, h: (b, 0, h)),
        out_shape=jax.ShapeDtypeStruct((b, seq, RET_HEADS * RET_DV_PAD), BF16),
        scratch_shapes=([pltpu.VMEM((seq, LANE), BF16) for _ in range(6)]
                        + [pltpu.VMEM((seq, RET_DV_PAD), BF16),
                           pltpu.VMEM((2, chunk, chunk), BF16),
                           pltpu.VMEM((seq, RET_DV_PAD), F32), pltpu.VMEM((seq, RET_DV_PAD), F32),
                           pltpu.VMEM((RET_DK, RET_DV_PAD), F32), pltpu.VMEM((RET_DK, RET_DV_PAD), F32)]),
        compiler_params=_cparams(2),
        name="retention_mixer",
    )(p3, p3, p3, p3, cos2, sin2, log_gamma, norm_g)


def _fold_time(u_ref, slab_ref, n_rows):
    n_slabs = S5_WIDTH // LANE
    for s in range(n_slabs):
        slab_ref[s] = u_ref[:, s * LANE:(s + 1) * LANE]
    pieces = [slab_ref[s, pl.ds(j, n_rows, stride=S5_CHUNK), :]
              for j in range(S5_CHUNK) for s in range(n_slabs)]
    return jnp.concatenate(pieces, axis=1).astype(BF16)


def _unfold_time(y, slab_ref, o_ref, n_rows):
    n_slabs = S5_WIDTH // LANE
    for j in range(S5_CHUNK):
        for s in range(n_slabs):
            col = (j * n_slabs + s) * LANE
            slab_ref[s, pl.ds(j, n_rows, stride=S5_CHUNK), :] = y[:, col:col + LANE]
    for s in range(n_slabs):
        o_ref[:, s * LANE:(s + 1) * LANE] = slab_ref[s]


def _s5_state_kernel(u_ref, b_ref, lam_ref, h_ref, slab_ref, xs_ref, hs_ref, *, n_rows):
    u = _fold_time(u_ref, slab_ref, n_rows)
    x = jnp.dot(u, b_ref[...], preferred_element_type=F32)
    n_tiles = 2 * S5_GROUPS
    for t in range(n_tiles):
        xs_ref[t] = x[:, t * LANE:(t + 1) * LANE]
    half = S5_STATE
    row = lax.broadcasted_iota(jnp.int32, (n_rows, LANE), 0)
    lane = lax.broadcasted_iota(jnp.int32, (1, LANE), 1)
    sign = jnp.where(lane < half, -1.0, 1.0)

    def cmul(z, a_re, a_im_signed):
        return z * a_re + pltpu.roll(z, half, 1) * a_im_signed

    def group(g, carry):
        lam = lam_ref[g]
        for d in range(2):
            z = xs_ref[2 * g + d]
            a_re = lam[2 * d:2 * d + 1, :]
            a_im = lam[2 * d + 1:2 * d + 2, :] * sign
            step = 1
            while step < n_rows:
                if d == 0:
                    shifted = jnp.where(row >= step, pltpu.roll(z, step, 0), 0.0)
                else:
                    shifted = jnp.where(row < n_rows - step, pltpu.roll(z, n_rows - step, 0), 0.0)
                z = z + cmul(shifted, a_re, a_im)
                a_re, a_im = a_re * a_re - a_im * a_im, 2.0 * a_re * a_im
                step *= 2
            if d == 0:
                z = jnp.where(row >= 1, pltpu.roll(z, 1, 0), 0.0)
            else:
                z = jnp.where(row < n_rows - 1, pltpu.roll(z, n_rows - 1, 0), 0.0)
            hs_ref[2 * g + d] = z.astype(BF16)
        return carry

    lax.fori_loop(0, S5_GROUPS, group, 0)
    for t in range(n_tiles):
        h_ref[:, t * LANE:(t + 1) * LANE] = hs_ref[t]


def _s5_out_kernel(u_ref, h_ref, mt_ref, ct_ref, y_ref, slab_ref, *, n_rows):
    u = _fold_time(u_ref, slab_ref, n_rows)
    y = lax.dot_general(u, mt_ref[...], NT_DIMS, preferred_element_type=F32)
    y = y + lax.dot_general(h_ref[...], ct_ref[...], NT_DIMS, preferred_element_type=F32)
    _unfold_time(y, slab_ref, y_ref, n_rows)


def _s5_core(p3, m, bst, cst, lam):
    b, seq, _ = p3.shape
    n_rows = seq // S5_CHUNK
    n_state = bst.shape[1]
    su_spec = pl.BlockSpec((None, seq, S5_WIDTH), lambda bi: (bi, 0, ODD_BLK["su"] // 2))
    state_spec = pl.BlockSpec((None, n_rows, n_state), lambda bi: (bi, 0, 0))
    full = lambda a: pl.BlockSpec(a.shape, lambda bi: (0,) * a.ndim)
    slabs = pltpu.VMEM((S5_WIDTH // LANE, seq, LANE), F32)
    h = pl.pallas_call(
        functools.partial(_s5_state_kernel, n_rows=n_rows),
        grid=(b,),
        in_specs=[su_spec, full(bst), full(lam)],
        out_specs=state_spec,
        out_shape=jax.ShapeDtypeStruct((b, n_rows, n_state), BF16),
        scratch_shapes=[slabs, pltpu.VMEM((n_state // LANE, n_rows, LANE), F32),
                        pltpu.VMEM((n_state // LANE, n_rows, LANE), BF16)],
        compiler_params=_cparams(1),
        name="s5_state",
    )(p3, bst, lam)
    return pl.pallas_call(
        functools.partial(_s5_out_kernel, n_rows=n_rows),
        grid=(b,),
        in_specs=[su_spec, state_spec, full(m), full(cst)],
        out_specs=pl.BlockSpec((None, seq, S5_WIDTH), lambda bi: (bi, 0, 0)),
        out_shape=jax.ShapeDtypeStruct((b, seq, S5_WIDTH), F32),
        scratch_shapes=[slabs],
        compiler_params=_cparams(1),
        name="s5_out",
    )(p3, h, m, cst)


def _s5_post_kernel(y_ref, u_ref, d_ref, w_ref, b_ref, o_ref):
    y = y_ref[...] + d_ref[...] * u_ref[...]
    g = _gelu_tanh(y)
    z = jnp.dot(g.astype(BF16), w_ref[...], preferred_element_type=F32) + b_ref[...]
    o_ref[...] = (g * _sigmoid(z)).astype(o_ref.dtype)


def _s5_post(y2d, p2d, d_skip, glu_w, glu_b):
    t, w = y2d.shape
    row = lambda i: (i, 0)
    const = lambda i: (0, 0)
    return pl.pallas_call(
        _s5_post_kernel, grid=(t // ROW_TILE,),
        in_specs=[pl.BlockSpec((ROW_TILE, w), row),
                  pl.BlockSpec((ROW_TILE, w), lambda i: (i, ODD_BLK["su"] // 2)),
                  pl.BlockSpec((1, w), const), pl.BlockSpec((w, w), const), pl.BlockSpec((1, w), const)],
        out_specs=pl.BlockSpec((ROW_TILE, w), row),
        out_shape=jax.ShapeDtypeStruct((t, w), BF16),
        compiler_params=_cparams(1),
        name="s5_glu",
    )(y2d, p2d, d_skip.reshape(1, w), glu_w.astype(BF16), glu_b.reshape(1, w))


def _s5_operators(lam_re, lam_im, log_dt, b_re, b_im, c_re, c_im):
    L = S5_CHUNK
    steps = jnp.arange(L, dtype=F32)
    cre, cim = c_re.astype(F32), c_im.astype(F32)
    c_stack = jnp.concatenate([cre, cim], axis=-1)

    t_sum = 0.0
    bst_parts, cst_parts, lam_rows = [], [], []
    for d in range(2):
        lr = jnp.minimum(lam_re[d].astype(F32), -1e-4)
        li = lam_im[d].astype(F32)
        dt = jnp.exp(log_dt[d].astype(F32))[:, None]
        a, w = lr * dt, li * dt
        mag = jnp.exp(a)
        ar, ai = mag * jnp.cos(w), mag * jnp.sin(w)
        den = lr * lr + li * li
        nr = ar - 1.0
        cr = (nr * lr + ai * li) / den
        ci = (ai * lr - nr * li) / den
        br, bi = b_re.astype(F32).transpose(0, 2, 1), b_im.astype(F32).transpose(0, 2, 1)
        bbr = cr[:, None, :] * br - ci[:, None, :] * bi
        bbi = cr[:, None, :] * bi + ci[:, None, :] * br

        def power(lag, a=a, w=w):
            m_ = jnp.exp(a * lag)
            return m_ * jnp.cos(w * lag), m_ * jnp.sin(w * lag)

        jj = steps[None, :, None, None, None]
        ii = steps[None, None, None, :, None]
        lag = jnp.maximum(ii - jj, 0.0) if d == 0 else jnp.maximum(jj - ii, 0.0)
        valid = (ii >= jj) if d == 0 else (jj >= ii)
        a5, w5 = a[:, None, None, None, :], w[:, None, None, None, :]
        pr, pi = power(lag, a5, w5)
        b5r, b5i = bbr[:, None, :, None, :], bbi[:, None, :, None, :]
        sr = jnp.where(valid, pr * b5r - pi * b5i, 0.0)
        si = jnp.where(valid, pr * b5i + pi * b5r, 0.0)
        t_sum = t_sum + jnp.concatenate([sr, -si], axis=-1)

        lag_s = (L - 1.0 - steps) if d == 0 else steps
        pr, pi = power(lag_s[None, :, None, None], a[:, None, None, :], w[:, None, None, :])
        bst_parts += [pr * bbr[:, None] - pi * bbi[:, None], pr * bbi[:, None] + pi * bbr[:, None]]

        lag_c = (steps + 1.0) if d == 0 else (L - steps)
        pr, pi = power(lag_c[None, :, None, None], a[:, None, None, :], w[:, None, None, :])
        cst_parts += [cre[:, None] * pr - cim[:, None] * pi,
                      -cre[:, None] * pi - cim[:, None] * pr]

        pr_l, pi_l = power(float(L))
        lam_rows += [jnp.concatenate([pr_l, pr_l], -1), jnp.concatenate([pi_l, pi_l], -1)]

    m = jnp.einsum('gjpin,gqn->gjpiq', t_sum, c_stack, precision=HIGHEST)
    m = m.reshape(S5_GROUPS, L, S5_GROUP_CH, L * S5_GROUP_CH)
    bst = jnp.stack(bst_parts, axis=3).reshape(S5_GROUPS, L, S5_GROUP_CH, 4 * S5_STATE)
    cst = jnp.stack(cst_parts, axis=3).reshape(S5_GROUPS, L, S5_GROUP_CH, 4 * S5_STATE)
    lam = jnp.stack(lam_rows, 1)

    def embed(cat):
        g, l, p, w_ = cat.shape
        flat = cat.reshape(g * l * p, w_).astype(BF16)
        row_g = lax.broadcasted_iota(jnp.int32, (g * l * p, 1), 0) // (l * p)
        blocks = jnp.concatenate([jnp.where(row_g == h, flat, 0) for h in range(g)], axis=1)
        return blocks.reshape(g, l, p, g * w_).transpose(1, 0, 2, 3).reshape(l * g * p, g * w_)

    b_full = embed(bst)
    c_full_t = embed(cst)
    m_rows = embed(m)
    m_full_t = m_rows.T.reshape(S5_GROUPS, L, S5_GROUP_CH, L * S5_WIDTH)
    m_full_t = m_full_t.transpose(1, 0, 2, 3).reshape(L * S5_WIDTH, L * S5_WIDTH)
    return m_full_t.astype(BF16), b_full.astype(BF16), c_full_t.astype(BF16), lam


def _proj_ffn_kernel(x_ref, xp_ref, xn_ref, ya_ref, yap_ref, yan_ref, yb_ref, ybp_ref, ybn_ref,
                     wa_ref, wb_ref, g_ref, wup_ref, cw_ref, cb_ref, wdn_ref, gfin_ref,
                     o_ref, act_ref, *, tiles_per_seq, final_norm):
    tm = x_ref.shape[0]
    ext = tm + 2 * HALO
    i = pl.program_id(0)
    first = (i % tiles_per_seq) == 0
    last = (i % tiles_per_seq) == tiles_per_seq - 1
    rows3 = lambda p, m, n: jnp.concatenate([p[...], m[...], n[...]], axis=0)
    x1 = rows3(xp_ref, x_ref, xn_ref)
    x1 = x1 + jnp.dot(rows3(yap_ref, ya_ref, yan_ref), wa_ref[...], preferred_element_type=F32)
    x1 = x1 + jnp.dot(rows3(ybp_ref, yb_ref, ybn_ref), wb_ref[...], preferred_element_type=F32)
    r = lax.broadcasted_iota(jnp.int32, (ext, 1), 0)
    outside = ((r < HALO) & first) | ((r >= tm + HALO) & last)
    h = jnp.where(outside, 0.0, _rms(x1, g_ref[...])).astype(BF16)
    down = []
    for j0 in range(0, FFN_DIM, FFN_GROUP):
        width = min(FFN_GROUP, FFN_DIM - j0)
        for j in range(j0, j0 + width, FFN_CHUNK):
            gated = []
            for base in (0, FFN_DIM):
                cols = slice(base + j, base + j + FFN_CHUNK)
                u = jnp.dot(h, wup_ref[:, cols], preferred_element_type=F32)
                u_prev = pltpu.roll(u, 1, 0)[HALO:HALO + tm, :]
                u_next = pltpu.roll(u, ext - 1, 0)[HALO:HALO + tm, :]
                c = cb_ref[:, cols] + u_prev * cw_ref[0:1, cols]
                c = c + u[HALO:HALO + tm, :] * cw_ref[1:2, cols]
                c = c + u_next * cw_ref[2:3, cols]
                gated.append(c)
            act_ref[:, j - j0:j - j0 + FFN_CHUNK] = (_silu(gated[0]) * gated[1]).astype(BF16)
        down.append(jnp.dot(act_ref[:, 0:width], wdn_ref[j0:j0 + width, :], preferred_element_type=F32))
    out = x1[HALO:HALO + tm, :] + functools.reduce(lambda a, b: a + b, down)
    if final_norm:
        out = _rms(out, gfin_ref[...])
    o_ref[...] = out


def _proj_ffn(x2d, seq, ya, yb, wa, wb, g, w_up, conv_w, conv_b, w_down, g_final, final_norm):
    t, d = x2d.shape
    tm = min(FFN_ROW_TILE, seq)
    tiles_per_seq = seq // tm
    halos_per_tile = tm // HALO
    n_halo_blocks = t // HALO

    def tile3(width):
        return [pl.BlockSpec((tm, width), lambda i: (i, 0)),
                pl.BlockSpec((HALO, width), lambda i: (jnp.maximum(i * halos_per_tile - 1, 0), 0)),
                pl.BlockSpec((HALO, width),
                             lambda i: (jnp.minimum((i + 1) * halos_per_tile, n_halo_blocks - 1), 0))]

    def resident(a):
        return pl.BlockSpec(a.shape, lambda i: (0,) * a.ndim)

    kern = functools.partial(_proj_ffn_kernel, tiles_per_seq=tiles_per_seq, final_norm=final_norm)
    consts = [wa, wb, g.reshape(1, d), w_up.astype(BF16), conv_w, conv_b.reshape(1, 2 * FFN_DIM),
              w_down.astype(BF16), g_final.reshape(1, d)]
    return pl.pallas_call(
        kern, grid=(t // tm,),
        in_specs=tile3(d) + tile3(ya.shape[1]) + tile3(yb.shape[1]) + [resident(c) for c in consts],
        out_specs=pl.BlockSpec((tm, d), lambda i: (i, 0)),
        out_shape=jax.ShapeDtypeStruct((t, d), F32),
        scratch_shapes=[pltpu.VMEM((tm, FFN_GROUP), BF16)],
        compiler_params=_cparams(1),
        name="proj_conv_ffn",
    )(x2d, x2d, x2d, ya, ya, ya, yb, yb, yb, *consts)


def _pad_cols(w, width):
    return jnp.pad(w, ((0, 0), (0, width - w.shape[1])))


def _even_layer_mix(x2d, b, seq, norm_g, w_in, w_out, wa2, ba, gla_g, lb_f, lb_b, hgrn_g):
    sizes = (256, 256, 512, 512, 16, 16, 256, 256, 256, 512, 512)
    offs = np.concatenate([[0], np.cumsum(sizes)])
    gq, gk, gv, gr, glf, glb, hq, hzf, hzb, hi, hg = (slice(offs[i], offs[i + 1]) for i in range(11))
    w_perm = jnp.concatenate([w_in[:, s] for s in (gq, gk, gv, gr, hq, hzf, hzb, hi, hg, glf, glb)], axis=1)
    w_perm = _pad_cols(w_perm, EVEN_COLS_PADDED).astype(BF16)
    p = _norm_matmul(x2d, norm_g, w_perm).reshape(b, seq, EVEN_COLS_PADDED)

    def gate_w(direction):
        w = jnp.zeros((LANE, GLA_HEADS * GLA_DK), F32)
        w = w.at[direction * GLA_RANK:(direction + 1) * GLA_RANK, :].set(wa2[direction].astype(F32))
        return w.reshape(LANE, 2, LANE).transpose(1, 0, 2).astype(BF16)

    baf = ba[0].astype(F32).reshape(2, 1, LANE)
    bab = ba[1].astype(F32).reshape(2, 1, LANE)
    ya = _gla_mixer(p, gate_w(0), gate_w(1), baf, bab, gla_g.astype(F32).reshape(2, 1, 2 * LANE))
    yb = _hgrn_mixer(p, lb_f.reshape(2, 1, LANE), lb_b.reshape(2, 1, LANE),
                     hgrn_g.astype(F32).reshape(2, 1, 2 * LANE))
    t = b * seq
    n_a = GLA_HEADS * GLA_DV
    return ya.reshape(t, n_a), yb.reshape(t, -1), w_out[:n_a].astype(BF16), w_out[n_a:].astype(BF16)


def _pad_heads(w, axis):
    shape = list(w.shape)
    shape[axis:axis + 1] = [RET_HEADS, RET_DV]
    w = w.reshape(shape)
    pad = [(0, 0)] * w.ndim
    pad[axis + 1] = (0, RET_DV_PAD - RET_DV)
    w = jnp.pad(w, pad)
    shape[axis:axis + 2] = [RET_HEADS * RET_DV_PAD]
    return w.reshape(shape)


def _odd_layer_mix(x2d, b, seq, norm_g, w_in, w_out, ret_g, lam_re, lam_im, log_dt, b_re, b_im,
                   c_re, c_im, d_skip, glu_w, glu_b, rope, log_gamma):
    hk = RET_HEADS * RET_DK
    hv = RET_HEADS * RET_DV
    w_perm = jnp.concatenate([w_in[:, :2 * hk],
                              _pad_heads(w_in[:, 2 * hk:2 * hk + hv], 1),
                              _pad_heads(w_in[:, 2 * hk + hv:2 * hk + 2 * hv], 1),
                              w_in[:, 2 * hk + 2 * hv:]], axis=1).astype(BF16)
    p2d = _norm_matmul(x2d, norm_g, w_perm)
    p = p2d.reshape(b, seq, ODD_COLS_PADDED)
    g_pad = _pad_heads(ret_g.astype(F32), 0).reshape(RET_HEADS, 1, RET_DV_PAD)
    yc = _retention_mixer(p, rope[0], rope[1], log_gamma, g_pad)

    m, bst, cst, lam = _s5_operators(lam_re, lam_im, log_dt, b_re, b_im, c_re, c_im)
    y2d = _s5_core(p, m, bst, cst, lam).reshape(b * seq, S5_WIDTH)
    yd = _s5_post(y2d, p2d, d_skip.astype(F32), glu_w, glu_b.astype(F32))

    t = b * seq
    w_ret = _pad_heads(w_out[:hv], 0).astype(BF16)
    return yc.reshape(t, -1), yd, w_ret, w_out[hv:].astype(BF16)


def _rope_tables(seq):
    half = RET_DK // 2
    inv = ROPE_BASE ** (-jnp.arange(half, dtype=F32) / half)
    ang = jnp.arange(seq, dtype=F32)[:, None] * inv[None, :]
    cos, sin = jnp.cos(ang), jnp.sin(ang)
    return jnp.concatenate([cos, cos], -1), jnp.concatenate([-sin, sin], -1)


def _retention_log_gamma(chunk):
    hidx = jnp.arange(RET_HEADS, dtype=F32)
    lg = jnp.stack([jnp.log1p(-jnp.exp2(-5.0 - hidx)), jnp.log1p(-jnp.exp2(-5.5 - hidx))], axis=1)
    return jnp.broadcast_to(lg[:, :, None], (RET_HEADS, 2, chunk))


def _hgrn_lower_bounds(lb_logits):
    p = jax.nn.softmax(lb_logits.astype(F32), axis=1)
    return jnp.cumsum(p, axis=1) - p[:, :1]


def kernel(x, mix_norm_g, ffn_norm_g, final_norm_g, w_in_even, w_out_even, gla_wa2, gla_ba, gla_norm_g,
           hgrn_lb_logits, hgrn_norm_g, w_in_odd, w_out_odd, ret_norm_g, s5_lam_re, s5_lam_im, s5_log_dt,
           s5_b_re, s5_b_im, s5_c_re, s5_c_im, s5_d, s5_glu_w, s5_glu_b,
           ffn_w_up, ffn_conv_w, ffn_conv_b, ffn_w_down):
    b, seq, d = x.shape
    lbs = _hgrn_lower_bounds(hgrn_lb_logits)
    rope = _rope_tables(seq)
    log_gamma = _retention_log_gamma(min(RET_CHUNK, seq))
    x2d = x.reshape(b * seq, d)
    for layer in range(DEPTH):
        j = layer // 2
        if layer % 2 == 0:
            mix = _even_layer_mix(x2d, b, seq, mix_norm_g[layer], w_in_even[j], w_out_even[j], gla_wa2[j],
                                  gla_ba[j], gla_norm_g[j], lbs[0, j], lbs[1, j], hgrn_norm_g[j])
        else:
            mix = _odd_layer_mix(x2d, b, seq, mix_norm_g[layer], w_in_odd[j], w_out_odd[j], ret_norm_g[j],
                                 s5_lam_re[j], s5_lam_im[j], s5_log_dt[j], s5_b_re[j], s5_b_im[j],
                                 s5_c_re[j], s5_c_im[j], s5_d[j], s5_glu_w[j], s5_glu_b[j], rope, log_gamma)
        x2d = _proj_ffn(x2d, seq, *mix, ffn_norm_g[layer], ffn_w_up[layer], ffn_conv_w[layer],
                        ffn_conv_b[layer], ffn_w_down[layer], final_norm_g, final_norm=(layer == DEPTH - 1))
    return x2d.reshape(b, seq, d)
```
